```python
import jax, jax.numpy as jnp
from jax import lax
import numpy as np

D_MODEL = 1024
BATCH = 8
SEQ = 4096
DEPTH = 1

N_META = 16
BLOCK = 128
HEAD_DIM = 64
ROPE_THETA = 10000.0
EPS = 1e-6
NEG_INF = -1e30
SA_HEADS = 8
SA_KV_HEADS = 1
TOPK_MAX = 256
IDX_HEADS = 4
IDX_DIM = 64
SW_HEADS = 8
SW_KV_HEADS = 2
WINDOW = 128
D_FF = 2816

IN_SIZES = (
    SA_HEADS * HEAD_DIM,
    SA_KV_HEADS * HEAD_DIM,
    SA_KV_HEADS * HEAD_DIM,
    IDX_HEADS * IDX_DIM,
    IDX_DIM,
    IDX_HEADS,
    SW_HEADS * HEAD_DIM,
    SW_KV_HEADS * HEAD_DIM,
    SW_KV_HEADS * HEAD_DIM,
    D_MODEL,
    D_MODEL,
)
IN_WIDTH = int(sum(IN_SIZES))
IN_SPLITS = tuple(int(v) for v in np.cumsum(IN_SIZES)[:-1])

kernel_name = "hybrid_dsa_swa_sink_macaron_meta"


def rmsnorm(x, g):
    xf = x.astype(jnp.float32)
    y = xf * lax.rsqrt(jnp.mean(xf * xf, axis=-1, keepdims=True) + EPS)
    return (y * g.astype(jnp.float32)).astype(x.dtype)


def rope(x, pos):
    d = x.shape[-1]
    inv_freq = 1.0 / (ROPE_THETA ** (jnp.arange(0, d, 2, dtype=jnp.float32) / d))
    ang = pos.astype(jnp.float32)[:, None] * inv_freq[None, :]
    cos = jnp.cos(ang)[None, :, None, :].astype(x.dtype)
    sin = jnp.sin(ang)[None, :, None, :].astype(x.dtype)
    x1, x2 = x[..., : d // 2], x[..., d // 2:]
    return jnp.concatenate([x1 * cos - x2 * sin, x2 * cos + x1 * sin], axis=-1)


def swiglu(x, w_gate, w_up, w_down):
    return (jax.nn.silu(x @ w_gate) * (x @ w_up)) @ w_down


def sparse_branch(qa, ka, va, qi, ki, wi, pos, kvalid, topk):
    B, N = qa.shape[0], qa.shape[1]
    NC = N // BLOCK
    scale = HEAD_DIM ** -0.5

    def to_blocks(t):
        return jnp.moveaxis(t.reshape((B, NC, BLOCK) + t.shape[2:]), 1, 0)

    def one_block(args):
        qa_b, qi_b, wi_b, qpos = args
        rel = jax.nn.relu(jnp.einsum('bqhd,bsd->bqhs', qi_b, ki).astype(jnp.float32))
        score = jnp.einsum('bqhs,bqh->bqs', rel, wi_b.astype(jnp.float32))
        admissible = kvalid[None, :] & (pos[None, :] <= qpos[:, None])
        score = jnp.where(admissible[None], score, NEG_INF)
        top_val, top_idx = lax.top_k(score, topk)
        sel_ok = top_val > 0.5 * NEG_INF
        k_sel = jax.vmap(lambda a, i: a[i])(ka, top_idx)
        v_sel = jax.vmap(lambda a, i: a[i])(va, top_idx)
        s = jnp.einsum('bqhd,bqkd->bqhk', qa_b, k_sel).astype(jnp.float32) * scale
        s = jnp.where(sel_ok[:, :, None, :], s, NEG_INF)
        p = jax.nn.softmax(s, axis=-1).astype(va.dtype)
        o = jnp.einsum('bqhk,bqkd->bqhd', p, v_sel)
        return o.reshape(B, BLOCK, SA_HEADS * HEAD_DIM)

    out = lax.map(one_block, (to_blocks(qa), to_blocks(qi), to_blocks(wi), pos.reshape(NC, BLOCK)))
    return jnp.moveaxis(out, 0, 1).reshape(B, N, SA_HEADS * HEAD_DIM)


def swa_branch(qs, ks, vs, sinks, pos, kvalid):
    B, N = qs.shape[0], qs.shape[1]
    NC = N // BLOCK
    G = SW_HEADS // SW_KV_HEADS
    scale = HEAD_DIM ** -0.5
    q = qs.reshape(B, NC, BLOCK, SW_KV_HEADS, G, HEAD_DIM)
    k = ks.reshape(B, NC, BLOCK, SW_KV_HEADS, HEAD_DIM)
    v = vs.reshape(B, NC, BLOCK, SW_KV_HEADS, HEAD_DIM)
    shift = lambda t: jnp.concatenate([jnp.zeros_like(t[:, :1]), t[:, :-1]], axis=1)
    kb = jnp.concatenate([shift(k), k], axis=2)
    vb = jnp.concatenate([shift(v), v], axis=2)
    pos_c = pos.reshape(NC, BLOCK)
    val_c = kvalid.reshape(NC, BLOCK)
    prev_pos = jnp.concatenate([jnp.full((1, BLOCK), -1, pos.dtype), pos_c[:-1]], axis=0)
    prev_val = jnp.concatenate([jnp.zeros((1, BLOCK), bool), val_c[:-1]], axis=0)
    kpos = jnp.concatenate([prev_pos, pos_c], axis=1)
    kval = jnp.concatenate([prev_val, val_c], axis=1)
    diff = pos_c[:, :, None] - kpos[:, None, :]
    mask = kval[:, None, :] & (diff >= 0) & (diff < WINDOW)
    s = jnp.einsum('bcqkgd,bcskd->bckgqs', q, kb).astype(jnp.float32) * scale
    s = jnp.where(mask[None, :, None, None], s, NEG_INF)
    sink = sinks.astype(jnp.float32).reshape(1, 1, SW_KV_HEADS, G, 1, 1)
    m = jnp.maximum(jnp.max(s, axis=-1, keepdims=True), sink)
    e = jnp.exp(s - m)
    p = e / (jnp.sum(e, axis=-1, keepdims=True) + jnp.exp(sink - m))
    o = jnp.einsum('bckgqs,bcskd->bcqkgd', p.astype(vs.dtype), vb)
    return o.reshape(B, N, SW_HEADS * HEAD_DIM)


def setup_inputs(seed: int = 0) -> dict:
    key = jax.random.key(seed)
    ks = jax.random.split(key, 20)
    f32 = jnp.float32
    nrm = lambda k, shape, fan_in: jax.random.normal(k, shape, f32) * (fan_in ** -0.5)
    gain = lambda k, shape: 1.0 + 0.01 * jax.random.normal(k, shape, f32)
    L = DEPTH
    return {
        "x": jax.random.normal(ks[0], (BATCH, SEQ, D_MODEL), f32),
        "meta_tokens": jax.random.normal(ks[1], (N_META, D_MODEL), f32),
        "norm_ffn1": gain(ks[2], (L, D_MODEL)),
        "w_ffn1_gate": nrm(ks[3], (L, D_MODEL, D_FF), D_MODEL),
        "w_ffn1_up": nrm(ks[4], (L, D_MODEL, D_FF), D_MODEL),
        "w_ffn1_down": nrm(ks[5], (L, D_FF, D_MODEL), D_FF),
        "norm_mix": gain(ks[6], (L, D_MODEL)),
        "w_in": nrm(ks[7], (L, D_MODEL, IN_WIDTH), D_MODEL),
        "sinks": 0.5 * jax.random.normal(ks[8], (L, SW_HEADS), f32),
        "w_branch_sparse": nrm(ks[9], (L, SA_HEADS * HEAD_DIM, D_MODEL), SA_HEADS * HEAD_DIM),
        "w_branch_swa": nrm(ks[10], (L, SW_HEADS * HEAD_DIM, D_MODEL), SW_HEADS * HEAD_DIM),
        "w_out": nrm(ks[11], (L, D_MODEL, D_MODEL), D_MODEL),
        "norm_ffn2": gain(ks[12], (L, D_MODEL)),
        "w_ffn2_gate": nrm(ks[13], (L, D_MODEL, D_FF), D_MODEL),
        "w_ffn2_up": nrm(ks[14], (L, D_MODEL, D_FF), D_MODEL),
        "w_ffn2_down": nrm(ks[15], (L, D_FF, D_MODEL), D_FF),
        "norm_final": gain(ks[16], (D_MODEL,)),
    }


def reference(x, meta_tokens, norm_ffn1, w_ffn1_gate, w_ffn1_up, w_ffn1_down, norm_mix, w_in, sinks,
              w_branch_sparse, w_branch_swa, w_out, norm_ffn2, w_ffn2_gate, w_ffn2_up, w_ffn2_down, norm_final):
    B, S, D = x.shape
    topk = min(TOPK_MAX, S // 4)
    n_pad = BLOCK - N_META
    h = jnp.concatenate([
        jnp.zeros((B, n_pad, D), x.dtype),
        jnp.broadcast_to(meta_tokens.astype(x.dtype)[None], (B, N_META, D)),
        x,
    ], axis=1)
    N = h.shape[1]
    pos = jnp.arange(N, dtype=jnp.int32) - n_pad
    kvalid = pos >= 0
    rpos = jnp.maximum(pos, 0)
    idx_scale = (IDX_HEADS ** -0.5) * (IDX_DIM ** -0.5)

    for l in range(DEPTH):
        h = h + 0.5 * swiglu(rmsnorm(h, norm_ffn1[l]), w_ffn1_gate[l], w_ffn1_up[l], w_ffn1_down[l])

        u = rmsnorm(h, norm_mix[l])
        qa, ka, va, qi, ki, wi, qs, ksw, vsw, ga, gb = jnp.split(u @ w_in[l], IN_SPLITS, axis=-1)
        qa = rope(qa.reshape(B, N, SA_HEADS, HEAD_DIM), rpos)
        ka = rope(ka.reshape(B, N, SA_KV_HEADS, HEAD_DIM), rpos)[:, :, 0]
        va = va.reshape(B, N, HEAD_DIM)
        qi = rope(qi.reshape(B, N, IDX_HEADS, IDX_DIM), rpos)
        ki = rope(ki.reshape(B, N, 1, IDX_DIM), rpos)[:, :, 0]
        wi = wi * idx_scale
        qs = rope(qs.reshape(B, N, SW_HEADS, HEAD_DIM), rpos)
        ksw = rope(ksw.reshape(B, N, SW_KV_HEADS, HEAD_DIM), rpos)
        vsw = vsw.reshape(B, N, SW_KV_HEADS, HEAD_DIM)

        o_sparse = sparse_branch(qa, ka, va, qi, ki, wi, pos, kvalid, topk)
        o_swa = swa_branch(qs, ksw, vsw, sinks[l], pos, kvalid)

        merged = (jax.nn.sigmoid(ga) * (o_sparse @ w_branch_sparse[l])
                  + jax.nn.sigmoid(gb) * (o_swa @ w_branch_swa[l]))
        h = h + merged @ w_out[l]

        h = h + 0.5 * swiglu(rmsnorm(h, norm_ffn2[l]), w_ffn2_gate[l], w_ffn2_up[l], w_ffn2_down[l])

    return rmsnorm(h[:, BLOCK:], norm_final)
```

```python
import functools

import numpy as np
import jax
import jax.numpy as jnp
from jax import lax
from jax.experimental import pallas as pl
from jax.experimental.pallas import tpu as pltpu

N_META = 16
BLOCK = 128
HEAD_DIM = 64
ROPE_THETA = 10000.0
EPS = 1e-6
NEG_INF = -1e30
SA_HEADS = 8
TOPK_MAX = 256
IDX_HEADS = 4
IDX_DIM = 64
SW_HEADS = 8
SW_KV_HEADS = 2
WINDOW = 128
N_PAD = BLOCK - N_META

LANES = 128
SUBLANES = 8
KEY_CHUNK = 256
VMEM_LIMIT_BYTES = 56 * 1024 * 1024

Q_COLS = SA_HEADS * HEAD_DIM + SW_HEADS * HEAD_DIM + IDX_HEADS * IDX_DIM
QS_OFF = SA_HEADS * HEAD_DIM
QI_OFF = QS_OFF + SW_HEADS * HEAD_DIM
V_COLS = 256
WI_COLS = LANES
KT_ROWS = HEAD_DIM + IDX_DIM + SW_KV_HEADS * HEAD_DIM
SEL_FLOOR = float(np.nextafter(np.float32(0.5 * NEG_INF), np.float32(0.0)))

_f32 = jnp.float32
_bf16 = jnp.bfloat16
_i32 = jnp.int32


def _rms(x, g):
    return x * lax.rsqrt(jnp.mean(x * x, axis=-1, keepdims=True) + EPS) * g


def _swiglu_half_step(x, g_norm, wg_ref, wu_ref, wd_ref):
    a = _rms(x, g_norm).astype(_bf16)
    g = jnp.dot(a, wg_ref[...], preferred_element_type=_f32)
    u = jnp.dot(a, wu_ref[...], preferred_element_type=_f32)
    act = (g * jax.nn.sigmoid(g) * u).astype(_bf16)
    return x + 0.5 * jnp.dot(act, wd_ref[...], preferred_element_type=_f32)


def _front_kernel(x_ref, cos_ref, sin_ref, cost_ref, sint_ref, g1_ref, gm_ref,
                  wg_ref, wu_ref, wd_ref, wrow_ref, wkt_ref,
                  h1_ref, q_ref, v_ref, wi_ref, gate_ref, kt_ref):
    h1 = _swiglu_half_step(x_ref[...], g1_ref[...], wg_ref, wu_ref, wd_ref)
    h1_ref[...] = h1
    u = _rms(h1, gm_ref[...]).astype(_bf16)
    p = jnp.dot(u, wrow_ref[...], preferred_element_type=_f32)

    pq = p[:, :Q_COLS]
    lane = lax.broadcasted_iota(_i32, pq.shape, 1)
    first_half = (lane & (HEAD_DIM - 1)) < HEAD_DIM // 2
    swapped = jnp.where(first_half,
                        pltpu.roll(pq, Q_COLS - HEAD_DIM // 2, 1),
                        pltpu.roll(pq, HEAD_DIM // 2, 1))
    reps = Q_COLS // LANES
    cos = jnp.concatenate([cos_ref[...]] * reps, axis=1)
    sin = jnp.concatenate([sin_ref[...]] * reps, axis=1)
    q_ref[...] = (pq * cos + swapped * sin).astype(_bf16)
    v_ref[...] = p[:, Q_COLS:Q_COLS + V_COLS].astype(_bf16)
    wi_ref[...] = p[:, Q_COLS + V_COLS:Q_COLS + V_COLS + WI_COLS]
    gate_ref[...] = jax.nn.sigmoid(p[:, Q_COLS + V_COLS + WI_COLS:])

    kt = lax.dot_general(wkt_ref[...], u, (((1,), (1,)), ((), ())),
                         preferred_element_type=_f32)
    cost = cost_ref[...]
    sint = sint_ref[...]
    half = HEAD_DIM // 2
    for h in range(KT_ROWS // HEAD_DIM):
        blk = kt[h * HEAD_DIM:(h + 1) * HEAD_DIM]
        sw = jnp.concatenate([blk[half:], blk[:half]], axis=0)
        kt_ref[h * HEAD_DIM:(h + 1) * HEAD_DIM, :] = (blk * cost + sw * sint).astype(_bf16)


def _const_spec(shape):
    nd = len(shape)
    return pl.BlockSpec(shape, lambda *_: (0,) * nd, pipeline_mode=pl.Buffered(1))


def _front(xs, tabs, g1, gm, wg, wu, wd, wrow, wkt, *, tm):
    B, S, D = xs.shape
    F = wg.shape[1]
    nt = S // tm
    row_cos, row_sin, col_cos, col_sin = tabs
    gate_cols = wrow.shape[1] - (Q_COLS + V_COLS + WI_COLS)
    row_blk = lambda w: pl.BlockSpec((None, tm, w), lambda b, j: (b, j, 0))
    out_shape = (
        jax.ShapeDtypeStruct((B, S, D), _f32),
        jax.ShapeDtypeStruct((B, S, Q_COLS), _bf16),
        jax.ShapeDtypeStruct((B, S, V_COLS), _bf16),
        jax.ShapeDtypeStruct((B, S, WI_COLS), _f32),
        jax.ShapeDtypeStruct((B, S, gate_cols), _f32),
        jax.ShapeDtypeStruct((B, KT_ROWS, S), _bf16),
    )
    return pl.pallas_call(
        _front_kernel,
        out_shape=out_shape,
        grid=(B, nt),
        in_specs=[
            row_blk(D),
            pl.BlockSpec((tm, LANES), lambda b, j: (j, 0)),
            pl.BlockSpec((tm, LANES), lambda b, j: (j, 0)),
            pl.BlockSpec((HEAD_DIM, tm), lambda b, j: (0, j)),
            pl.BlockSpec((HEAD_DIM, tm), lambda b, j: (0, j)),
            _const_spec((1, D)), _const_spec((1, D)),
            _const_spec((D, F)), _const_spec((D, F)), _const_spec((F, D)),
            _const_spec(wrow.shape), _const_spec(wkt.shape),
        ],
        out_specs=(
            row_blk(D), row_blk(Q_COLS), row_blk(V_COLS), row_blk(WI_COLS), row_blk(gate_cols),
            pl.BlockSpec((None, KT_ROWS, tm), lambda b, j: (b, 0, j)),
        ),
        compiler_params=pltpu.CompilerParams(
            dimension_semantics=("arbitrary", "arbitrary"),
            vmem_limit_bytes=VMEM_LIMIT_BYTES),
        name="front",
    )(xs, row_cos, row_sin, col_cos, col_sin, g1, gm, wg, wu, wd, wrow, wkt)


def _key_to_float(u):
    key = u ^ jnp.int32(-2 ** 31)
    bits = jnp.where(key >= 0, key, key ^ jnp.int32(0x7FFFFFFF))
    return lax.bitcast_convert_type(bits, _f32)


def _sublane_allsum(x):
    x = x + pltpu.roll(x, 4, 0)
    x = x + pltpu.roll(x, 2, 0)
    return x + pltpu.roll(x, 1, 0)


def _lanes_from_sublanes(x8):
    return jnp.broadcast_to(x8[:1], (LANES, LANES)).T


def _attn_kernel(q_ref, wi_ref, kt_ref, v_ref, sink_ref, osp_ref, osw_ref,
                 s_scr, st_scr, qh_scr, m_scr, l_scr, acc_scr, *, topk):
    tq = BLOCK
    i = pl.program_id(1)
    nch = (i + 3) // 2
    vregs = KEY_CHUNK // SUBLANES
    row = lax.broadcasted_iota(_i32, (tq, 1), 0)
    colq = BLOCK * (i + 1) + row

    wi = wi_ref[...]
    qi = [q_ref[:, QI_OFF + h * IDX_DIM:QI_OFF + (h + 1) * IDX_DIM] for h in range(IDX_HEADS)]

    def index_body(c, carry):
        off = pl.multiple_of(c * KEY_CHUNK, KEY_CHUNK)
        kic = kt_ref[HEAD_DIM:HEAD_DIM + IDX_DIM, pl.ds(off, KEY_CHUNK)]
        acc = jnp.zeros((tq, KEY_CHUNK), _f32)
        for h in range(IDX_HEADS):
            d = jnp.dot(qi[h], kic, preferred_element_type=_f32)
            acc = acc + jnp.maximum(d, 0.0) * wi[:, h:h + 1]
        col = off + lax.broadcasted_iota(_i32, (tq, KEY_CHUNK), 1)
        sc = jnp.where((col >= N_PAD) & (col <= colq), acc, NEG_INF)
        s_scr[:, pl.ds(off, KEY_CHUNK)] = sc
        st_scr[pl.ds(off, LANES), :] = sc[:, :LANES].T
        st_scr[pl.ds(off + LANES, LANES), :] = sc[:, LANES:].T
        return carry

    lax.fori_loop(0, nch, index_body, 0)

    def count_where(pred):
        def body(c, acc):
            off = pl.multiple_of(c * KEY_CHUNK, KEY_CHUNK)
            st = st_scr[pl.ds(off, KEY_CHUNK), :].reshape(vregs // 4, 4, SUBLANES, LANES)
            kidx = (off + lax.broadcasted_iota(_i32, (vregs // 4, 4, SUBLANES, LANES), 0) * (4 * SUBLANES)
                    + lax.broadcasted_iota(_i32, (vregs // 4, 4, SUBLANES, LANES), 1) * SUBLANES
                    + lax.broadcasted_iota(_i32, (vregs // 4, 4, SUBLANES, LANES), 2))
            return acc + jnp.sum(jnp.where(pred(st, kidx), 1, 0).astype(_i32), axis=0)
        acc = lax.fori_loop(0, nch, body, jnp.zeros((4, SUBLANES, LANES), _i32))
        return _sublane_allsum(acc[0] + acc[1] + acc[2] + acc[3])

    def bit_body(it, u8):
        cand = u8 | lax.shift_left(jnp.int32(1), 31 - it)
        t8 = _key_to_float(cand)
        cnt = count_where(lambda st, kidx: st >= t8)
        return jnp.where(cnt >= topk, cand, u8)

    u8 = lax.fori_loop(0, 32, bit_body, jnp.zeros((SUBLANES, LANES), _i32))
    t8 = _key_to_float(u8)

    cnt_gt = count_where(lambda st, kidx: st > t8)
    cnt_ge = count_where(lambda st, kidx: st >= t8)
    need = topk - cnt_gt
    tie_rows = (cnt_ge - cnt_gt > need) & (t8 >= SEL_FLOOR)
    has_ties = jnp.max(jnp.where(tie_rows, 1, 0).astype(_i32))
    tb = jnp.maximum(_lanes_from_sublanes(t8), SEL_FLOOR)

    @pl.when(has_ties > 0)
    def _():
        def jbit_body(it, j8):
            cand = j8 | lax.shift_left(jnp.int32(1), 13 - it)
            cnt = count_where(lambda st, kidx: (st == t8) & (kidx < cand))
            return jnp.where(cnt <= need, cand, j8)
        j8 = lax.fori_loop(0, 14, jbit_body, jnp.zeros((SUBLANES, LANES), _i32))
        jb = _lanes_from_sublanes(j8)
        tbe = _lanes_from_sublanes(t8)

        def fix_body(c, carry):
            off = pl.multiple_of(c * LANES, LANES)
            s = s_scr[:, pl.ds(off, LANES)]
            col = off + lax.broadcasted_iota(_i32, (tq, LANES), 1)
            s_scr[:, pl.ds(off, LANES)] = jnp.where((s == tbe) & (col >= jb), NEG_INF, s)
            return carry
        lax.fori_loop(0, 2 * nch, fix_body, 0)

    for h in range(SA_HEADS):
        qh_scr[h] = q_ref[:, h * HEAD_DIM:(h + 1) * HEAD_DIM]
    m_scr[...] = jnp.full(m_scr.shape, NEG_INF, _f32)
    l_scr[...] = jnp.zeros(l_scr.shape, _f32)
    acc_scr[...] = jnp.zeros(acc_scr.shape, _f32)
    tb2 = jnp.concatenate([tb, tb], axis=1)

    def attn_body(c, carry):
        off = pl.multiple_of(c * KEY_CHUNK, KEY_CHUNK)
        kac = kt_ref[0:HEAD_DIM, pl.ds(off, KEY_CHUNK)]
        vac = v_ref[pl.ds(off, KEY_CHUNK), 0:HEAD_DIM]
        sel = s_scr[:, pl.ds(off, KEY_CHUNK)] >= tb2
        s = jnp.dot(qh_scr[...].reshape(SA_HEADS * tq, HEAD_DIM), kac,
                    preferred_element_type=_f32).reshape(SA_HEADS, tq, KEY_CHUNK)
        s = jnp.where(sel[None], s, NEG_INF)
        m_old = m_scr[...]
        m_new = jnp.maximum(m_old, jnp.max(s, axis=-1, keepdims=True))
        alpha = jnp.exp(m_old - m_new)
        p = jnp.exp(s - m_new)
        l_scr[...] = alpha * l_scr[...] + jnp.sum(p, axis=-1, keepdims=True)
        pv = jnp.dot(p.astype(_bf16).reshape(SA_HEADS * tq, KEY_CHUNK), vac,
                     preferred_element_type=_f32).reshape(SA_HEADS, tq, HEAD_DIM)
        acc_scr[...] = alpha * acc_scr[...] + pv
        m_scr[...] = m_new
        return carry

    lax.fori_loop(0, nch, attn_body, 0)
    o = acc_scr[...] / l_scr[...]
    osp_ref[...] = jnp.concatenate([o[h] for h in range(SA_HEADS)], axis=1).astype(_bf16)

    koff = pl.multiple_of(i * BLOCK, BLOCK)
    jcol = lax.broadcasted_iota(_i32, (tq, 2 * BLOCK), 1)
    diff = BLOCK + row - jcol
    wmask = (koff + jcol >= N_PAD) & (diff >= 0) & (diff < WINDOW)
    groups = SW_HEADS // SW_KV_HEADS
    outs = []
    for g in range(SW_KV_HEADS):
        kc = kt_ref[HEAD_DIM + IDX_DIM + g * HEAD_DIM:HEAD_DIM + IDX_DIM + (g + 1) * HEAD_DIM,
                    pl.ds(koff, 2 * BLOCK)]
        vc = v_ref[pl.ds(koff, 2 * BLOCK), (1 + g) * HEAD_DIM:(2 + g) * HEAD_DIM]
        qg = jnp.concatenate(
            [q_ref[:, QS_OFF + (g * groups + j) * HEAD_DIM:QS_OFF + (g * groups + j + 1) * HEAD_DIM]
             for j in range(groups)], axis=0)
        s = jnp.dot(qg, kc, preferred_element_type=_f32).reshape(groups, tq, 2 * BLOCK)
        s = jnp.where(wmask[None], s, NEG_INF)
        sink = sink_ref[g * groups:(g + 1) * groups][:, :, :1]
        m = jnp.maximum(jnp.max(s, axis=-1, keepdims=True), sink)
        e = jnp.exp(s - m)
        p = e / (jnp.sum(e, axis=-1, keepdims=True) + jnp.exp(sink - m))
        og = jnp.dot(p.astype(_bf16).reshape(groups * tq, 2 * BLOCK), vc,
                     preferred_element_type=_f32).reshape(groups, tq, HEAD_DIM)
        outs += [og[j] for j in range(groups)]
    osw_ref[...] = jnp.concatenate(outs, axis=1).astype(_bf16)


def _attention(q, wi, kt, v, sink_b, *, topk):
    B, S, _ = q.shape
    nkp = kt.shape[2]
    tq = BLOCK
    kernel = functools.partial(_attn_kernel, topk=topk)
    out_blk = pl.BlockSpec((None, tq, SA_HEADS * HEAD_DIM), lambda b, i: (b, i, 0))
    return pl.pallas_call(
        kernel,
        out_shape=(jax.ShapeDtypeStruct((B, S, SA_HEADS * HEAD_DIM), _bf16),
                   jax.ShapeDtypeStruct((B, S, SW_HEADS * HEAD_DIM), _bf16)),
        grid=(B, S // tq),
        in_specs=[
            pl.BlockSpec((None, tq, Q_COLS), lambda b, i: (b, i, 0)),
            pl.BlockSpec((None, tq, WI_COLS), lambda b, i: (b, i, 0)),
            pl.BlockSpec((None, KT_ROWS, nkp), lambda b, i: (b, 0, 0)),
            pl.BlockSpec((None, nkp, V_COLS), lambda b, i: (b, 0, 0)),
            pl.BlockSpec(sink_b.shape, lambda b, i: (0, 0, 0)),
        ],
        out_specs=(out_blk, out_blk),
        scratch_shapes=[
            pltpu.VMEM((tq, nkp), _f32),
            pltpu.VMEM((nkp, tq), _f32),
            pltpu.VMEM((SA_HEADS, tq, HEAD_DIM), _bf16),
            pltpu.VMEM((SA_HEADS, tq, 1), _f32),
            pltpu.VMEM((SA_HEADS, tq, 1), _f32),
            pltpu.VMEM((SA_HEADS, tq, HEAD_DIM), _f32),
        ],
        compiler_params=pltpu.CompilerParams(
            dimension_semantics=("arbitrary", "arbitrary"),
            vmem_limit_bytes=VMEM_LIMIT_BYTES),
        name="attn",
    )(q, wi, kt, v, sink_b)


def _back_kernel(h1_ref, osp_ref, osw_ref, gate_ref, wbs_ref, wbw_ref, wo_ref, g2_ref,
                 wg_ref, wu_ref, wd_ref, gf_ref, out_ref):
    d = h1_ref.shape[-1]
    a = jnp.dot(osp_ref[...], wbs_ref[...], preferred_element_type=_f32)
    b = jnp.dot(osw_ref[...], wbw_ref[...], preferred_element_type=_f32)
    gates = gate_ref[...]
    merged = (gates[:, :d] * a + gates[:, d:] * b).astype(_bf16)
    h2 = h1_ref[...] + jnp.dot(merged, wo_ref[...], preferred_element_type=_f32)
    h3 = _swiglu_half_step(h2, g2_ref[...], wg_ref, wu_ref, wd_ref)
    out_ref[...] = _rms(h3, gf_ref[...])


def _back(h1, osp, osw, gates, wbs, wbw, wo, g2, wg, wu, wd, gf, *, tm):
    B, S, D = h1.shape
    F = wg.shape[1]
    row_blk = lambda w: pl.BlockSpec((None, tm, w), lambda b, j: (b, j, 0))
    return pl.pallas_call(
        _back_kernel,
        out_shape=jax.ShapeDtypeStruct((B, S, D), _f32),
        grid=(B, S // tm),
        in_specs=[
            row_blk(D), row_blk(osp.shape[-1]), row_blk(osw.shape[-1]), row_blk(gates.shape[-1]),
            _const_spec(wbs.shape), _const_spec(wbw.shape), _const_spec(wo.shape), _const_spec((1, D)),
            _const_spec((D, F)), _const_spec((D, F)), _const_spec((F, D)), _const_spec((1, D)),
        ],
        out_specs=row_blk(D),
        compiler_params=pltpu.CompilerParams(
            dimension_semantics=("arbitrary", "arbitrary"),
            vmem_limit_bytes=VMEM_LIMIT_BYTES),
        name="back",
    )(h1, osp, osw, gates, wbs, wbw, wo, g2, wg, wu, wd, gf)


def _rope_tables(rpos):
    inv_freq = 1.0 / (ROPE_THETA ** (jnp.arange(0, HEAD_DIM, 2, dtype=_f32) / HEAD_DIM))
    ang = rpos.astype(_f32)[:, None] * inv_freq[None, :]
    cos, sin = jnp.cos(ang), jnp.sin(ang)
    cos64 = jnp.concatenate([cos, cos], axis=1)
    sin64 = jnp.concatenate([-sin, sin], axis=1)
    reps = LANES // HEAD_DIM
    return (jnp.tile(cos64, (1, reps)), jnp.tile(sin64, (1, reps)), cos64.T, sin64.T)


def _pack_w_in(w):
    sizes = (SA_HEADS * HEAD_DIM, HEAD_DIM, HEAD_DIM, IDX_HEADS * IDX_DIM, IDX_DIM, IDX_HEADS,
             SW_HEADS * HEAD_DIM, SW_KV_HEADS * HEAD_DIM, SW_KV_HEADS * HEAD_DIM)
    d = w.shape[0]
    gate_cols = w.shape[1] - sum(sizes)
    splits = [int(s) for s in np.cumsum(sizes + (gate_cols,))[:-1]]
    qa, ka, va, qi, ki, wi, qs, ksw, vsw, gates = jnp.split(w, splits, axis=1)
    scale = HEAD_DIM ** -0.5
    idx_scale = (IDX_HEADS ** -0.5) * (IDX_DIM ** -0.5)
    zeros = lambda n: jnp.zeros((d, n), w.dtype)
    wrow = jnp.concatenate([
        qa * scale, qs * scale, qi,
        va, vsw, zeros(V_COLS - va.shape[1] - vsw.shape[1]),
        wi * idx_scale, zeros(WI_COLS - wi.shape[1]),
        gates], axis=1)
    wkt = jnp.concatenate([ka, ki, ksw], axis=1).T
    return wrow.astype(_bf16), wkt.astype(_bf16)


def kernel(x, meta_tokens, norm_ffn1, w_ffn1_gate, w_ffn1_up, w_ffn1_down, norm_mix, w_in, sinks,
           w_branch_sparse, w_branch_swa, w_out, norm_ffn2, w_ffn2_gate, w_ffn2_up, w_ffn2_down, norm_final):
    B, S, D = x.shape
    assert norm_ffn1.shape[0] == 1, "single-layer stack only"
    assert S % BLOCK == 0 and IDX_HEADS * IDX_DIM + SA_HEADS * HEAD_DIM + SW_HEADS * HEAD_DIM == Q_COLS
    topk = min(TOPK_MAX, S // 4)
    tm = 256 if S % 256 == 0 else BLOCK
    bf = lambda a: a.astype(_bf16)

    wrow, wkt = _pack_w_in(w_in[0])
    g1, gm, g2 = norm_ffn1[0][None], norm_mix[0][None], norm_ffn2[0][None]
    wg1, wu1, wd1 = bf(w_ffn1_gate[0]), bf(w_ffn1_up[0]), bf(w_ffn1_down[0])

    pos_real = N_META + jnp.arange(S, dtype=jnp.int32)
    pos_meta = jnp.maximum(jnp.arange(BLOCK, dtype=jnp.int32) - N_PAD, 0)
    h1, q, v, wi, gates, kt = _front(x, _rope_tables(pos_real), g1, gm, wg1, wu1, wd1, wrow, wkt, tm=tm)
    meta_blk = jnp.concatenate([jnp.zeros((N_PAD, D), x.dtype), meta_tokens.astype(x.dtype)], axis=0)[None]
    _, _, v_m, _, _, kt_m = _front(meta_blk, _rope_tables(pos_meta), g1, gm, wg1, wu1, wd1, wrow, wkt, tm=BLOCK)

    nkp = -(-(S + BLOCK) // KEY_CHUNK) * KEY_CHUNK
    kt_all = jnp.concatenate([jnp.broadcast_to(kt_m, (B, KT_ROWS, BLOCK)), kt,
                              jnp.zeros((B, KT_ROWS, nkp - S - BLOCK), kt.dtype)], axis=2)
    v_all = jnp.concatenate([jnp.broadcast_to(v_m, (B, BLOCK, V_COLS)), v,
                             jnp.zeros((B, nkp - S - BLOCK, V_COLS), v.dtype)], axis=1)
    sink_b = jnp.broadcast_to(sinks[0].astype(_f32)[:, None, None], (SW_HEADS, 1, LANES))

    o_sparse, o_swa = _attention(q, wi, kt_all, v_all, sink_b, topk=topk)

    return _back(h1, o_sparse, o_swa, gates, bf(w_branch_sparse[0]), bf(w_branch_swa[0]), bf(w_out[0]), g2,
                 bf(w_ffn2_gate[0]), bf(w_ffn2_up[0]), bf(w_ffn2_down[0]), norm_final[None], tm=tm)
```

```python
import functools

import numpy as np
import jax
import jax.numpy as jnp
from jax import lax
from jax.experimental import pallas as pl
from jax.experimental.pallas import tpu as pltpu

N_META = 16
BLOCK = 128
HEAD_DIM = 64
ROPE_THETA = 10000.0
EPS = 1e-6
NEG_INF = -1e30
SA_HEADS = 8
TOPK_MAX = 256
IDX_HEADS = 4
IDX_DIM = 64
SW_HEADS = 8
SW_KV_HEADS = 2
WINDOW = 128
N_PAD = BLOCK - N_META

LANES = 128
SUBLANES = 8
KEY_CHUNK = 256
VMEM_LIMIT_BYTES = 56 * 1024 * 1024

Q_COLS = SA_HEADS * HEAD_DIM + SW_HEADS * HEAD_DIM + IDX_HEADS * IDX_DIM
QS_OFF = SA_HEADS * HEAD_DIM
QI_OFF = QS_OFF + SW_HEADS * HEAD_DIM
V_GROUP = 2 * HEAD_DIM
V_COLS = (1 + SW_KV_HEADS) * V_GROUP
WI_COLS = LANES
KT_ROWS = HEAD_DIM + IDX_DIM + SW_KV_HEADS * HEAD_DIM
SEL_FLOOR = float(np.nextafter(np.float32(0.5 * NEG_INF), np.float32(0.0)))

_f32 = jnp.float32
_bf16 = jnp.bfloat16
_i32 = jnp.int32


def _rms(x, g):
    return x * lax.rsqrt(jnp.mean(x * x, axis=-1, keepdims=True) + EPS) * g


def _swiglu_half_step(x, g_norm, wg_ref, wu_ref, wd_ref):
    a = _rms(x, g_norm).astype(_bf16)
    g = jnp.dot(a, wg_ref[...], preferred_element_type=_f32)
    u = jnp.dot(a, wu_ref[...], preferred_element_type=_f32)
    act = (g * jax.nn.sigmoid(g) * u).astype(_bf16)
    return x + 0.5 * jnp.dot(act, wd_ref[...], preferred_element_type=_f32)


def _front_kernel(x_ref, cos_ref, sin_ref, cost_ref, sint_ref, g1_ref, gm_ref,
                  wg_ref, wu_ref, wd_ref, wrow_ref, wkt_ref,
                  h1_ref, q_ref, v_ref, wi_ref, gate_ref, kt_ref):
    h1 = _swiglu_half_step(x_ref[...], g1_ref[...], wg_ref, wu_ref, wd_ref)
    h1_ref[...] = h1
    u = _rms(h1, gm_ref[...]).astype(_bf16)
    p = jnp.dot(u, wrow_ref[...], preferred_element_type=_f32)

    pq = p[:, :Q_COLS]
    lane = lax.broadcasted_iota(_i32, pq.shape, 1)
    first_half = (lane & (HEAD_DIM - 1)) < HEAD_DIM // 2
    swapped = jnp.where(first_half,
                        pltpu.roll(pq, Q_COLS - HEAD_DIM // 2, 1),
                        pltpu.roll(pq, HEAD_DIM // 2, 1))
    reps = Q_COLS // LANES
    cos = jnp.concatenate([cos_ref[...]] * reps, axis=1)
    sin = jnp.concatenate([sin_ref[...]] * reps, axis=1)
    q_ref[...] = (pq * cos + swapped * sin).astype(_bf16)
    vlane = lax.broadcasted_iota(_i32, (p.shape[0], V_COLS), 1)
    ones_col = jnp.where((vlane & (V_GROUP - 1)) == HEAD_DIM, 1.0, 0.0)
    v_ref[...] = (p[:, Q_COLS:Q_COLS + V_COLS] + ones_col).astype(_bf16)
    wi_ref[...] = p[:, Q_COLS + V_COLS:Q_COLS + V_COLS + WI_COLS]
    gate_ref[...] = jax.nn.sigmoid(p[:, Q_COLS + V_COLS + WI_COLS:])

    kt = lax.dot_general(wkt_ref[...], u, (((1,), (1,)), ((), ())),
                         preferred_element_type=_f32)
    cost = cost_ref[...]
    sint = sint_ref[...]
    half = HEAD_DIM // 2
    for h in range(KT_ROWS // HEAD_DIM):
        blk = kt[h * HEAD_DIM:(h + 1) * HEAD_DIM]
        sw = jnp.concatenate([blk[half:], blk[:half]], axis=0)
        kt_ref[h * HEAD_DIM:(h + 1) * HEAD_DIM, :] = (blk * cost + sw * sint).astype(_bf16)


def _const_spec(shape):
    nd = len(shape)
    return pl.BlockSpec(shape, lambda *_: (0,) * nd, pipeline_mode=pl.Buffered(1))


def _front(xs, tabs, g1, gm, wg, wu, wd, wrow, wkt, *, tm):
    B, S, D = xs.shape
    F = wg.shape[1]
    nt = S // tm
    row_cos, row_sin, col_cos, col_sin = tabs
    gate_cols = wrow.shape[1] - (Q_COLS + V_COLS + WI_COLS)
    row_blk = lambda w: pl.BlockSpec((None, tm, w), lambda b, j: (b, j, 0))
    out_shape = (
        jax.ShapeDtypeStruct((B, S, D), _f32),
        jax.ShapeDtypeStruct((B, S, Q_COLS), _bf16),
        jax.ShapeDtypeStruct((B, S, V_COLS), _bf16),
        jax.ShapeDtypeStruct((B, S, WI_COLS), _f32),
        jax.ShapeDtypeStruct((B, S, gate_cols), _f32),
        jax.ShapeDtypeStruct((B, KT_ROWS, S), _bf16),
    )
    return pl.pallas_call(
        _front_kernel,
        out_shape=out_shape,
        grid=(B, nt),
        in_specs=[
            row_blk(D),
            pl.BlockSpec((tm, LANES), lambda b, j: (j, 0)),
            pl.BlockSpec((tm, LANES), lambda b, j: (j, 0)),
            pl.BlockSpec((HEAD_DIM, tm), lambda b, j: (0, j)),
            pl.BlockSpec((HEAD_DIM, tm), lambda b, j: (0, j)),
            _const_spec((1, D)), _const_spec((1, D)),
            _const_spec((D, F)), _const_spec((D, F)), _const_spec((F, D)),
            _const_spec(wrow.shape), _const_spec(wkt.shape),
        ],
        out_specs=(
            row_blk(D), row_blk(Q_COLS), row_blk(V_COLS), row_blk(WI_COLS), row_blk(gate_cols),
            pl.BlockSpec((None, KT_ROWS, tm), lambda b, j: (b, 0, j)),
        ),
        compiler_params=pltpu.CompilerParams(
            dimension_semantics=("arbitrary", "arbitrary"),
            vmem_limit_bytes=VMEM_LIMIT_BYTES),
        name="front",
    )(xs, row_cos, row_sin, col_cos, col_sin, g1, gm, wg, wu, wd, wrow, wkt)


def _key_to_float(u):
    key = u ^ jnp.int32(-2 ** 31)
    bits = jnp.where(key >= 0, key, key ^ jnp.int32(0x7FFFFFFF))
    return lax.bitcast_convert_type(bits, _f32)


def _sublane_allsum(x):
    x = x + pltpu.roll(x, 4, 0)
    x = x + pltpu.roll(x, 2, 0)
    return x + pltpu.roll(x, 1, 0)


def _lanes_from_sublanes(x8):
    return jnp.broadcast_to(x8[:1], (LANES, LANES)).T


def _attn_kernel(q_ref, wi_ref, kt_ref, v_ref, sink_ref, osp_ref, osw_ref,
                 s_scr, st_scr, qh_scr, sall_scr, mx_scr, acc_scr, *, topk):
    tq = BLOCK
    i = pl.program_id(1)
    nch = (i + 3) // 2
    vregs = KEY_CHUNK // SUBLANES
    row = lax.broadcasted_iota(_i32, (tq, 1), 0)
    colq = BLOCK * (i + 1) + row

    wi = wi_ref[...]
    qi = [q_ref[:, QI_OFF + h * IDX_DIM:QI_OFF + (h + 1) * IDX_DIM] for h in range(IDX_HEADS)]

    def index_body(c, carry):
        off = pl.multiple_of(c * KEY_CHUNK, KEY_CHUNK)
        kic = kt_ref[HEAD_DIM:HEAD_DIM + IDX_DIM, pl.ds(off, KEY_CHUNK)]
        acc = jnp.zeros((tq, KEY_CHUNK), _f32)
        for h in range(IDX_HEADS):
            d = jnp.dot(qi[h], kic, preferred_element_type=_f32)
            acc = acc + jnp.maximum(d, 0.0) * wi[:, h:h + 1]
        col = off + lax.broadcasted_iota(_i32, (tq, KEY_CHUNK), 1)
        sc = jnp.where((col >= N_PAD) & (col <= colq), acc, NEG_INF)
        s_scr[:, pl.ds(off, KEY_CHUNK)] = sc
        st_scr[pl.ds(off, LANES), :] = sc[:, :LANES].T
        st_scr[pl.ds(off + LANES, LANES), :] = sc[:, LANES:].T
        return carry

    lax.fori_loop(0, nch, index_body, 0)

    def count_where(pred):
        def body(c, acc):
            off = pl.multiple_of(c * KEY_CHUNK, KEY_CHUNK)
            st = st_scr[pl.ds(off, KEY_CHUNK), :].reshape(vregs // 4, 4, SUBLANES, LANES)
            kidx = (off + lax.broadcasted_iota(_i32, (vregs // 4, 4, SUBLANES, LANES), 0) * (4 * SUBLANES)
                    + lax.broadcasted_iota(_i32, (vregs // 4, 4, SUBLANES, LANES), 1) * SUBLANES
                    + lax.broadcasted_iota(_i32, (vregs // 4, 4, SUBLANES, LANES), 2))
            return acc + jnp.sum(jnp.where(pred(st, kidx), 1, 0).astype(_i32), axis=0)
        acc = lax.fori_loop(0, nch, body, jnp.zeros((4, SUBLANES, LANES), _i32))
        return _sublane_allsum(acc[0] + acc[1] + acc[2] + acc[3])

    def bit_body(it, u8):
        cand = u8 | lax.shift_left(jnp.int32(1), 31 - it)
        t8 = _key_to_float(cand)
        cnt = count_where(lambda st, kidx: st >= t8)
        return jnp.where(cnt >= topk, cand, u8)

    u8 = lax.fori_loop(0, 32, bit_body, jnp.zeros((SUBLANES, LANES), _i32))
    t8 = _key_to_float(u8)

    cnt_gt = count_where(lambda st, kidx: st > t8)
    cnt_ge = count_where(lambda st, kidx: st >= t8)
    need = topk - cnt_gt
    tie_rows = (cnt_ge - cnt_gt > need) & (t8 >= SEL_FLOOR)
    has_ties = jnp.max(jnp.where(tie_rows, 1, 0).astype(_i32))
    tb = jnp.maximum(_lanes_from_sublanes(t8), SEL_FLOOR)

    @pl.when(has_ties > 0)
    def _():
        def jbit_body(it, j8):
            cand = j8 | lax.shift_left(jnp.int32(1), 13 - it)
            cnt = count_where(lambda st, kidx: (st == t8) & (kidx < cand))
            return jnp.where(cnt <= need, cand, j8)
        j8 = lax.fori_loop(0, 14, jbit_body, jnp.zeros((SUBLANES, LANES), _i32))
        jb = _lanes_from_sublanes(j8)
        tbe = _lanes_from_sublanes(t8)

        def fix_body(c, carry):
            off = pl.multiple_of(c * LANES, LANES)
            s = s_scr[:, pl.ds(off, LANES)]
            col = off + lax.broadcasted_iota(_i32, (tq, LANES), 1)
            s_scr[:, pl.ds(off, LANES)] = jnp.where((s == tbe) & (col >= jb), NEG_INF, s)
            return carry
        lax.fori_loop(0, 2 * nch, fix_body, 0)

    for h in range(SA_HEADS):
        qh_scr[h] = q_ref[:, h * HEAD_DIM:(h + 1) * HEAD_DIM]
    mx_scr[...] = jnp.full(mx_scr.shape, NEG_INF, _f32)
    acc_scr[...] = jnp.zeros(acc_scr.shape, _f32)
    tb2 = jnp.concatenate([tb, tb], axis=1)

    def score_body(c, carry):
        off = pl.multiple_of(c * KEY_CHUNK, KEY_CHUNK)
        kac = kt_ref[0:HEAD_DIM, pl.ds(off, KEY_CHUNK)]
        sidx = s_scr[:, pl.ds(off, KEY_CHUNK)]
        for h in range(SA_HEADS):
            s = jnp.dot(qh_scr[h], kac, preferred_element_type=_f32)
            s = jnp.where(sidx >= tb2, s, NEG_INF)
            sall_scr[h, :, pl.ds(off, KEY_CHUNK)] = s
            mx_scr[h] = jnp.maximum(mx_scr[h], jnp.maximum(s[:, :LANES], s[:, LANES:]))
        return carry

    lax.fori_loop(0, nch, score_body, 0)
    for h in range(SA_HEADS):
        mx_scr[h] = jnp.broadcast_to(jnp.max(mx_scr[h], axis=-1, keepdims=True), (tq, LANES))

    def pv_body(c, carry):
        off = pl.multiple_of(c * KEY_CHUNK, KEY_CHUNK)
        vac = v_ref[pl.ds(off, KEY_CHUNK), 0:V_GROUP]
        for h in range(SA_HEADS):
            mb = mx_scr[h]
            s = sall_scr[h, :, pl.ds(off, KEY_CHUNK)]
            p = jnp.exp(s - jnp.concatenate([mb, mb], axis=1)).astype(_bf16)
            acc_scr[h] += jnp.dot(p, vac, preferred_element_type=_f32)
        return carry

    lax.fori_loop(0, nch, pv_body, 0)
    outs = []
    for h in range(SA_HEADS):
        a = acc_scr[h]
        outs.append(a[:, :HEAD_DIM] / a[:, HEAD_DIM:HEAD_DIM + 1])
    osp_ref[...] = jnp.concatenate(outs, axis=1).astype(_bf16)

    koff = pl.multiple_of(i * BLOCK, BLOCK)
    jcol = lax.broadcasted_iota(_i32, (tq, 2 * BLOCK), 1)
    diff = BLOCK + row - jcol
    wmask = (koff + jcol >= N_PAD) & (diff >= 0) & (diff < WINDOW)
    groups = SW_HEADS // SW_KV_HEADS
    outs = []
    for g in range(SW_KV_HEADS):
        kc = kt_ref[HEAD_DIM + IDX_DIM + g * HEAD_DIM:HEAD_DIM + IDX_DIM + (g + 1) * HEAD_DIM,
                    pl.ds(koff, 2 * BLOCK)]
        vc = v_ref[pl.ds(koff, 2 * BLOCK), (1 + g) * V_GROUP:(2 + g) * V_GROUP]
        for j in range(groups):
            hq = g * groups + j
            qj = q_ref[:, QS_OFF + hq * HEAD_DIM:QS_OFF + (hq + 1) * HEAD_DIM]
            s = jnp.where(wmask, jnp.dot(qj, kc, preferred_element_type=_f32), NEG_INF)
            sink = sink_ref[hq][:, :1]
            m = jnp.maximum(jnp.max(s, axis=-1, keepdims=True), sink)
            e = jnp.exp(s - m).astype(_bf16)
            ev = jnp.dot(e, vc, preferred_element_type=_f32)
            outs.append(ev[:, :HEAD_DIM] / (ev[:, HEAD_DIM:HEAD_DIM + 1] + jnp.exp(sink - m)))
    osw_ref[...] = jnp.concatenate(outs, axis=1).astype(_bf16)


def _attention(q, wi, kt, v, sink_b, *, topk):
    B, S, _ = q.shape
    nkp = kt.shape[2]
    tq = BLOCK
    kernel = functools.partial(_attn_kernel, topk=topk)
    out_blk = pl.BlockSpec((None, tq, SA_HEADS * HEAD_DIM), lambda b, i: (b, i, 0))
    return pl.pallas_call(
        kernel,
        out_shape=(jax.ShapeDtypeStruct((B, S, SA_HEADS * HEAD_DIM), _bf16),
                   jax.ShapeDtypeStruct((B, S, SW_HEADS * HEAD_DIM), _bf16)),
        grid=(B, S // tq),
        in_specs=[
            pl.BlockSpec((None, tq, Q_COLS), lambda b, i: (b, i, 0)),
            pl.BlockSpec((None, tq, WI_COLS), lambda b, i: (b, i, 0)),
            pl.BlockSpec((None, KT_ROWS, nkp), lambda b, i: (b, 0, 0)),
            pl.BlockSpec((None, nkp, V_COLS), lambda b, i: (b, 0, 0)),
            pl.BlockSpec(sink_b.shape, lambda b, i: (0, 0, 0)),
        ],
        out_specs=(out_blk, out_blk),
        scratch_shapes=[
            pltpu.VMEM((tq, nkp), _f32),
            pltpu.VMEM((nkp, tq), _f32),
            pltpu.VMEM((SA_HEADS, tq, HEAD_DIM), _bf16),
            pltpu.VMEM((SA_HEADS, tq, nkp), _f32),
            pltpu.VMEM((SA_HEADS, tq, LANES), _f32),
            pltpu.VMEM((SA_HEADS, tq, V_GROUP), _f32),
        ],
        compiler_params=pltpu.CompilerParams(
            dimension_semantics=("arbitrary", "arbitrary"),
            vmem_limit_bytes=VMEM_LIMIT_BYTES),
        name="attn",
    )(q, wi, kt, v, sink_b)


def _back_kernel(h1_ref, osp_ref, osw_ref, gate_ref, wbs_ref, wbw_ref, wo_ref, g2_ref,
                 wg_ref, wu_ref, wd_ref, gf_ref, out_ref):
    d = h1_ref.shape[-1]
    a = jnp.dot(osp_ref[...], wbs_ref[...], preferred_element_type=_f32)
    b = jnp.dot(osw_ref[...], wbw_ref[...], preferred_element_type=_f32)
    gates = gate_ref[...]
    merged = (gates[:, :d] * a + gates[:, d:] * b).astype(_bf16)
    h2 = h1_ref[...] + jnp.dot(merged, wo_ref[...], preferred_element_type=_f32)
    h3 = _swiglu_half_step(h2, g2_ref[...], wg_ref, wu_ref, wd_ref)
    out_ref[...] = _rms(h3, gf_ref[...])


def _back(h1, osp, osw, gates, wbs, wbw, wo, g2, wg, wu, wd, gf, *, tm):
    B, S, D = h1.shape
    F = wg.shape[1]
    row_blk = lambda w: pl.BlockSpec((None, tm, w), lambda b, j: (b, j, 0))
    return pl.pallas_call(
        _back_kernel,
        out_shape=jax.ShapeDtypeStruct((B, S, D), _f32),
        grid=(B, S // tm),
        in_specs=[
            row_blk(D), row_blk(osp.shape[-1]), row_blk(osw.shape[-1]), row_blk(gates.shape[-1]),
            _const_spec(wbs.shape), _const_spec(wbw.shape), _const_spec(wo.shape), _const_spec((1, D)),
            _const_spec((D, F)), _const_spec((D, F)), _const_spec((F, D)), _const_spec((1, D)),
        ],
        out_specs=row_blk(D),
        compiler_params=pltpu.CompilerParams(
            dimension_semantics=("arbitrary", "arbitrary"),
            vmem_limit_bytes=VMEM_LIMIT_BYTES),
        name="back",
    )(h1, osp, osw, gates, wbs, wbw, wo, g2, wg, wu, wd, gf)


def _rope_tables(rpos):
    inv_freq = 1.0 / (ROPE_THETA ** (jnp.arange(0, HEAD_DIM, 2, dtype=_f32) / HEAD_DIM))
    ang = rpos.astype(_f32)[:, None] * inv_freq[None, :]
    cos, sin = jnp.cos(ang), jnp.sin(ang)
    cos64 = jnp.concatenate([cos, cos], axis=1)
    sin64 = jnp.concatenate([-sin, sin], axis=1)
    reps = LANES // HEAD_DIM
    return (jnp.tile(cos64, (1, reps)), jnp.tile(sin64, (1, reps)), cos64.T, sin64.T)


def _pack_w_in(w):
    sizes = (SA_HEADS * HEAD_DIM, HEAD_DIM, HEAD_DIM, IDX_HEADS * IDX_DIM, IDX_DIM, IDX_HEADS,
             SW_HEADS * HEAD_DIM, SW_KV_HEADS * HEAD_DIM, SW_KV_HEADS * HEAD_DIM)
    d = w.shape[0]
    gate_cols = w.shape[1] - sum(sizes)
    splits = [int(s) for s in np.cumsum(sizes + (gate_cols,))[:-1]]
    qa, ka, va, qi, ki, wi, qs, ksw, vsw, gates = jnp.split(w, splits, axis=1)
    scale = HEAD_DIM ** -0.5
    idx_scale = (IDX_HEADS ** -0.5) * (IDX_DIM ** -0.5)
    zeros = lambda n: jnp.zeros((d, n), w.dtype)
    wrow = jnp.concatenate([
        qa * scale, qs * scale, qi,
        va, zeros(HEAD_DIM),
        *[blk for g in range(SW_KV_HEADS)
          for blk in (vsw[:, g * HEAD_DIM:(g + 1) * HEAD_DIM], zeros(HEAD_DIM))],
        wi * idx_scale, zeros(WI_COLS - wi.shape[1]),
        gates], axis=1)
    wkt = jnp.concatenate([ka, ki, ksw], axis=1).T
    return wrow.astype(_bf16), wkt.astype(_bf16)


def kernel(x, meta_tokens, norm_ffn1, w_ffn1_gate, w_ffn1_up, w_ffn1_down, norm_mix, w_in, sinks,
           w_branch_sparse, w_branch_swa, w_out, norm_ffn2, w_ffn2_gate, w_ffn2_up, w_ffn2_down, norm_final):
    B, S, D = x.shape
    assert norm_ffn1.shape[0] == 1, "single-layer stack only"
    assert S % BLOCK == 0 and IDX_HEADS * IDX_DIM + SA_HEADS * HEAD_DIM + SW_HEADS * HEAD_DIM == Q_COLS
    topk = min(TOPK_MAX, S // 4)
    tm = 256 if S % 256 == 0 else BLOCK
    bf = lambda a: a.astype(_bf16)

    wrow, wkt = _pack_w_in(w_in[0])
    g1, gm, g2 = norm_ffn1[0][None], norm_mix[0][None], norm_ffn2[0][None]
    wg1, wu1, wd1 = bf(w_ffn1_gate[0]), bf(w_ffn1_up[0]), bf(w_ffn1_down[0])

    pos_real = N_META + jnp.arange(S, dtype=jnp.int32)
    pos_meta = jnp.maximum(jnp.arange(BLOCK, dtype=jnp.int32) - N_PAD, 0)
    h1, q, v, wi, gates, kt = _front(x, _rope_tables(pos_real), g1, gm, wg1, wu1, wd1, wrow, wkt, tm=tm)
    meta_blk = jnp.concatenate([jnp.zeros((N_PAD, D), x.dtype), meta_tokens.astype(x.dtype)], axis=0)[None]
    _, _, v_m, _, _, kt_m = _front(meta_blk, _rope_tables(pos_meta), g1, gm, wg1, wu1, wd1, wrow, wkt, tm=BLOCK)

    nkp = -(-(S + BLOCK) // KEY_CHUNK) * KEY_CHUNK
    kt_all = jnp.concatenate([jnp.broadcast_to(kt_m, (B, KT_ROWS, BLOCK)), kt,
                              jnp.zeros((B, KT_ROWS, nkp - S - BLOCK), kt.dtype)], axis=2)
    v_all = jnp.concatenate([jnp.broadcast_to(v_m, (B, BLOCK, V_COLS)), v,
                             jnp.zeros((B, nkp - S - BLOCK, V_COLS), v.dtype)], axis=1)
    sink_b = jnp.broadcast_to(sinks[0].astype(_f32)[:, None, None], (SW_HEADS, 1, LANES))

    o_sparse, o_swa = _attention(q, wi, kt_all, v_all, sink_b, topk=topk)

    return _back(h1, o_sparse, o_swa, gates, bf(w_branch_sparse[0]), bf(w_branch_swa[0]), bf(w_out[0]), g2,
                 bf(w_ffn2_gate[0]), bf(w_ffn2_up[0]), bf(w_ffn2_down[0]), norm_final[None], tm=tm)
```

```python
import functools

import numpy as np
import jax
import jax.numpy as jnp
from jax import lax
from jax.experimental import pallas as pl
from jax.experimental.pallas import tpu as pltpu

N_META = 16
BLOCK = 128
HEAD_DIM = 64
ROPE_THETA = 10000.0
EPS = 1e-6
NEG_INF = -1e30
SA_HEADS = 8
TOPK_MAX = 256
IDX_HEADS = 4
IDX_DIM = 64
SW_HEADS = 8
SW_KV_HEADS = 2
WINDOW = 128
N_PAD = BLOCK - N_META
SW_GROUP = SW_HEADS // SW_KV_HEADS
LOG2E = 1.4426950408889634

LANES = 128
SUBLANES = 8
PACK16 = 16
KEY_CHUNK = 256
SEARCH_CHUNK = 2 * KEY_CHUNK
VMEM_LIMIT_BYTES = 56 * 1024 * 1024

QT_ROWS = (SA_HEADS + SW_HEADS + IDX_HEADS) * HEAD_DIM
QS_ROW = SA_HEADS * HEAD_DIM
QI_ROW = QS_ROW + SW_HEADS * HEAD_DIM
V_GROUP = HEAD_DIM + PACK16
VT_ROWS = (1 + SW_KV_HEADS) * V_GROUP
WI_ROWS = PACK16
PT_ROWS = QT_ROWS + VT_ROWS + WI_ROWS
K_COLS = 4 * HEAD_DIM
SEL_FLOOR = float(np.nextafter(np.float32(0.5 * NEG_INF), np.float32(0.0)))
KEY_BELOW_ALL = int(np.int32(np.uint32(0xFF800000) ^ np.uint32(0x7FFFFFFF)))
KEY_ABOVE_ALL = 0x7F800001
BISECT_STEPS_PER_CHECK = 4

_f32 = jnp.float32
_bf16 = jnp.bfloat16
_i32 = jnp.int32


def _rms(x, g):
    return x * lax.rsqrt(jnp.mean(x * x, axis=-1, keepdims=True) + EPS) * g


def _swiglu_half_step(x, g_norm, wg_ref, wu_ref, wd_ref):
    a = _rms(x, g_norm).astype(_bf16)
    g = jnp.dot(a, wg_ref[...], preferred_element_type=_f32)
    u = jnp.dot(a, wu_ref[...], preferred_element_type=_f32)
    act = (g * jax.nn.sigmoid(g) * u).astype(_bf16)
    return x + 0.5 * jnp.dot(act, wd_ref[...], preferred_element_type=_f32)


def _front_kernel(x_ref, cos_ref, sin_ref, cost_ref, sint_ref, g1_ref, gm_ref,
                  wg_ref, wu_ref, wd_ref, wrow_ref, wt_ref,
                  h1_ref, k_ref, gate_ref, qt_ref, vt_ref, wit_ref):
    h1 = _swiglu_half_step(x_ref[...], g1_ref[...], wg_ref, wu_ref, wd_ref)
    h1_ref[...] = h1
    u = _rms(h1, gm_ref[...]).astype(_bf16)
    half = HEAD_DIM // 2

    p = jnp.dot(u, wrow_ref[...], preferred_element_type=_f32)
    pk = p[:, :K_COLS]
    lane = lax.broadcasted_iota(_i32, pk.shape, 1)
    swapped = jnp.where((lane & (HEAD_DIM - 1)) < half,
                        pltpu.roll(pk, K_COLS - half, 1), pltpu.roll(pk, half, 1))
    reps = K_COLS // LANES
    cos = jnp.concatenate([cos_ref[...]] * reps, axis=1)
    sin = jnp.concatenate([sin_ref[...]] * reps, axis=1)
    k_ref[...] = (pk * cos + swapped * sin).astype(_bf16)
    gate_ref[...] = jax.nn.sigmoid(p[:, K_COLS:])

    pt = lax.dot_general(wt_ref[...], u, (((1,), (1,)), ((), ())),
                         preferred_element_type=_f32)
    tm = pt.shape[1]
    nq = QT_ROWS // HEAD_DIM
    q3 = pt[:QT_ROWS].reshape(nq, HEAD_DIM, tm)
    sw = jnp.concatenate([q3[:, half:], q3[:, :half]], axis=1)
    q3 = q3 * cost_ref[...][None] + sw * sint_ref[...][None]
    head = lax.broadcasted_iota(_i32, (nq, 1, 1), 0)
    q3 = q3 * jnp.where(head < SA_HEADS + SW_HEADS, LOG2E, 1.0)
    qt_ref[...] = q3.reshape(QT_ROWS, tm).astype(_bf16)
    vrow = lax.broadcasted_iota(_i32, (VT_ROWS, tm), 0)
    ones_row = jnp.zeros((VT_ROWS, tm), _f32)
    for g in range(VT_ROWS // V_GROUP):
        ones_row = jnp.where(vrow == g * V_GROUP + HEAD_DIM, 1.0, ones_row)
    vt_ref[...] = (pt[QT_ROWS:QT_ROWS + VT_ROWS] + ones_row).astype(_bf16)
    wit_ref[...] = pt[QT_ROWS + VT_ROWS:]


def _const_spec(shape):
    nd = len(shape)
    return pl.BlockSpec(shape, lambda *_: (0,) * nd, pipeline_mode=pl.Buffered(1))


def _front(xs, tabs, g1, gm, wg, wu, wd, wrow, wt, *, tm):
    B, S, D = xs.shape
    F = wg.shape[1]
    row_cos, row_sin, col_cos, col_sin = tabs
    gate_cols = wrow.shape[1] - K_COLS
    row_blk = lambda w: pl.BlockSpec((None, tm, w), lambda b, j: (b, j, 0))
    col_blk = lambda r: pl.BlockSpec((None, r, tm), lambda b, j: (b, 0, j))
    out_shape = (
        jax.ShapeDtypeStruct((B, S, D), _f32),
        jax.ShapeDtypeStruct((B, S, K_COLS), _bf16),
        jax.ShapeDtypeStruct((B, S, gate_cols), _f32),
        jax.ShapeDtypeStruct((B, QT_ROWS, S), _bf16),
        jax.ShapeDtypeStruct((B, VT_ROWS, S), _bf16),
        jax.ShapeDtypeStruct((B, WI_ROWS, S), _f32),
    )
    return pl.pallas_call(
        _front_kernel,
        out_shape=out_shape,
        grid=(B, S // tm),
        in_specs=[
            row_blk(D),
            pl.BlockSpec((tm, LANES), lambda b, j: (j, 0)),
            pl.BlockSpec((tm, LANES), lambda b, j: (j, 0)),
            pl.BlockSpec((HEAD_DIM, tm), lambda b, j: (0, j)),
            pl.BlockSpec((HEAD_DIM, tm), lambda b, j: (0, j)),
            _const_spec((1, D)), _const_spec((1, D)),
            _const_spec((D, F)), _const_spec((D, F)), _const_spec((F, D)),
            _const_spec(wrow.shape), _const_spec(wt.shape),
        ],
        out_specs=(row_blk(D), row_blk(K_COLS), row_blk(gate_cols),
                   col_blk(QT_ROWS), col_blk(VT_ROWS), col_blk(WI_ROWS)),
        compiler_params=pltpu.CompilerParams(
            dimension_semantics=("arbitrary", "arbitrary"),
            vmem_limit_bytes=VMEM_LIMIT_BYTES),
        name="front",
    )(xs, row_cos, row_sin, col_cos, col_sin, g1, gm, wg, wu, wd, wrow, wt)


def _sortable(bits):
    return bits ^ (lax.shift_right_arithmetic(bits, 31) & jnp.int32(0x7FFFFFFF))


def _key_to_float(key):
    return lax.bitcast_convert_type(_sortable(key), _f32)


def _float_to_key(x):
    return _sortable(lax.bitcast_convert_type(x, _i32))


def _sublane_all(x, op):
    x = op(x, pltpu.roll(x, 4, 0))
    x = op(x, pltpu.roll(x, 2, 0))
    return op(x, pltpu.roll(x, 1, 0))


def _tree(parts, op):
    while len(parts) > 1:
        parts = [op(parts[j], parts[j + 1]) if j + 1 < len(parts) else parts[j] for j in range(0, len(parts), 2)]
    return parts[0]


def _attn_kernel(sink_ref, qt_ref, wit_ref, k_ref, vt_ref, osp_ref, osw_ref,
                 st_scr, wq_scr, rawa_scr, rawb_scr, pa_scr, pb_scr, acc_scr, *, topk):
    tq = BLOCK
    i = pl.program_id(1)
    nch = (i + 3) // 2
    nsl = (nch + 1) // 2
    n32 = KEY_CHUNK // SUBLANES
    s32 = SEARCH_CHUNK // SUBLANES
    krow = lax.broadcasted_iota(_i32, (KEY_CHUNK, tq), 0)
    srow = lax.broadcasted_iota(_i32, (SEARCH_CHUNK, tq), 0)
    qcol = BLOCK * (i + 1) + lax.broadcasted_iota(_i32, (KEY_CHUNK, tq), 1)

    def chunk_off(c):
        return pl.multiple_of(c * KEY_CHUNK, KEY_CHUNK)

    def slab_off(c):
        return pl.multiple_of(c * SEARCH_CHUNK, SEARCH_CHUNK)

    def pair_cols(pair):
        return slice(2 * pair * tq, 2 * (pair + 1) * tq)


    zero_panel = jnp.zeros((HEAD_DIM, tq), _bf16)
    for h in range(IDX_HEADS):
        wq_scr[0:HEAD_DIM, h * tq:(h + 1) * tq] = zero_panel
        wq_scr[HEAD_DIM:, h * tq:(h + 1) * tq] = qt_ref[QI_ROW + h * HEAD_DIM:QI_ROW + (h + 1) * HEAD_DIM, :]
    wi = wit_ref[...]
    wrow = jnp.concatenate([jnp.broadcast_to(wi[h:h + 1], (SUBLANES, tq)) for h in range(IDX_HEADS)], axis=1)

    def index_dots(c, raw_scr):
        kc = k_ref[pl.ds(chunk_off(c), KEY_CHUNK), 0:2 * HEAD_DIM]
        for pair in range(IDX_HEADS // 2):
            raw_scr[:, pair_cols(pair)] = jnp.dot(kc, wq_scr[:, pair_cols(pair)], preferred_element_type=_f32)

    def index_finish(c, raw_scr):
        off = chunk_off(c)
        terms = []
        for pair in range(IDX_HEADS // 2):
            d = jnp.maximum(raw_scr[:, pair_cols(pair)], 0.0).reshape(n32, SUBLANES, 2 * tq)
            d = d * wrow[None, :, pair_cols(pair)]
            terms += [d[:, :, :tq], d[:, :, tq:]]
        sc = _tree(terms, jnp.add).reshape(KEY_CHUNK, tq)
        kidx = off + krow
        st_scr[pl.ds(off, KEY_CHUNK), :] = jnp.where((kidx >= N_PAD) & (kidx <= qcol), sc, NEG_INF)

    last_chunk = 2 * nsl - 1

    def index_body(j, carry):
        index_dots(2 * j + 1, rawb_scr)
        index_finish(2 * j, rawa_scr)
        index_dots(jnp.minimum(2 * j + 2, last_chunk), rawa_scr)
        index_finish(2 * j + 1, rawb_scr)
        return carry

    index_dots(0, rawa_scr)
    lax.fori_loop(0, nsl, index_body, 0)

    def count32(pred):
        def body(c, acc):
            off = slab_off(c)
            st = st_scr[pl.ds(off, SEARCH_CHUNK), :].reshape(s32 // 4, 4, SUBLANES, tq)
            kidx = off + srow.reshape(s32 // 4, 4, SUBLANES, tq)
            ones = jnp.where(pred(st, kidx), 1, 0).astype(_i32)
            return acc + _tree([ones[j] for j in range(s32 // 4)], jnp.add)
        acc = lax.fori_loop(0, nsl, body, jnp.zeros((4, SUBLANES, tq), _i32))
        return _sublane_all(_tree([acc[j] for j in range(4)], jnp.add), jnp.add)

    def count_ge(key):
        t = _key_to_float(key)
        return count32(lambda st, kidx: st >= t)

    def bound_body(c, acc):
        return jnp.maximum(acc, st_scr[pl.ds(chunk_off(c), KEY_CHUNK), :].reshape(n32, SUBLANES, tq))
    tops = lax.fori_loop(0, 2 * nsl, bound_body, jnp.full((n32, SUBLANES, tq), NEG_INF, _f32))
    ub = _sublane_all(_tree([tops[j] for j in range(n32)], jnp.maximum), jnp.maximum)
    lb = _sublane_all(_tree([tops[j] for j in range(n32)], jnp.minimum), jnp.minimum)
    lo0 = _float_to_key(lb)
    hi0 = _float_to_key(ub) + 1
    lo = jnp.where(count_ge(lo0) >= topk, lo0, KEY_BELOW_ALL)
    hi = jnp.where(count_ge(hi0) < topk, hi0, KEY_ABOVE_ALL)

    def bisect(lo, hi):
        mid = lax.shift_right_arithmetic(lo, 1) + lax.shift_right_arithmetic(hi, 1) + (lo & hi & 1)
        cnt = count_ge(mid)
        ge = cnt >= topk
        return jnp.where(ge, mid, lo), jnp.where(cnt == topk, mid + 1, jnp.where(ge, hi, mid))

    def search_body(state):
        lo, hi, _ = state
        for _ in range(BISECT_STEPS_PER_CHECK):
            lo, hi = bisect(lo, hi)
        return lo, hi, jnp.max(jnp.where(hi - lo != 1, 1, 0).astype(_i32))

    lo, hi, _ = lax.while_loop(lambda state: state[2] > 0, search_body, (lo, hi, jnp.int32(1)))
    t8 = _key_to_float(lo)

    cnt_gt = count32(lambda st, kidx: st > t8)
    cnt_ge = count32(lambda st, kidx: st >= t8)
    need = topk - cnt_gt
    tie_rows = (cnt_ge - cnt_gt > need) & (t8 >= SEL_FLOOR)
    has_ties = jnp.max(jnp.where(tie_rows, 1, 0).astype(_i32))

    @pl.when(has_ties > 0)
    def _():
        def jbit_body(it, j8):
            cand = j8 | lax.shift_left(jnp.int32(1), 13 - it)
            cnt = count32(lambda st, kidx: (st == t8) & (kidx < cand))
            return jnp.where(cnt <= need, cand, j8)
        j8 = lax.fori_loop(0, 14, jbit_body, jnp.zeros((SUBLANES, tq), _i32))

        def fix_body(c, carry):
            off = slab_off(c)
            st = st_scr[pl.ds(off, SEARCH_CHUNK), :].reshape(s32, SUBLANES, tq)
            kidx = off + srow.reshape(s32, SUBLANES, tq)
            st_scr[pl.ds(off, SEARCH_CHUNK), :] = jnp.where(
                (st == t8[None]) & (kidx >= j8[None]), NEG_INF, st).reshape(SEARCH_CHUNK, tq)
            return carry
        lax.fori_loop(0, nsl, fix_body, 0)

    tsel = jnp.maximum(t8, SEL_FLOOR)

    for h in range(SA_HEADS):
        wq_scr[0:HEAD_DIM, h * tq:(h + 1) * tq] = qt_ref[h * HEAD_DIM:(h + 1) * HEAD_DIM, :]
        wq_scr[HEAD_DIM:, h * tq:(h + 1) * tq] = zero_panel
    pb_scr[...] = jnp.zeros(pb_scr.shape, _bf16)
    acc_scr[...] = jnp.zeros(acc_scr.shape, _f32)

    def score_dots(c, raw_scr):
        kc = k_ref[pl.ds(chunk_off(c), KEY_CHUNK), 0:2 * HEAD_DIM]
        for pair in range(SA_HEADS // 2):
            raw_scr[:, pair_cols(pair)] = jnp.dot(kc, wq_scr[:, pair_cols(pair)], preferred_element_type=_f32)

    def softmax_chunk(c, raw_scr, p_scr, m_old):
        sel = st_scr[pl.ds(chunk_off(c), KEY_CHUNK), :].reshape(n32, SUBLANES, tq) >= tsel[None]
        m_new, alpha = [], []
        for h in range(SA_HEADS):
            sh = jnp.where(sel, raw_scr[:, h * tq:(h + 1) * tq].reshape(n32, SUBLANES, tq), NEG_INF)
            mh = jnp.maximum(m_old[h], _sublane_all(_tree([sh[j] for j in range(n32)], jnp.maximum), jnp.maximum))
            p_scr[:, h * tq:(h + 1) * tq] = jnp.exp2(sh - mh[None]).reshape(KEY_CHUNK, tq).astype(_bf16)
            alpha.append(jnp.exp2(m_old[h] - mh))
            m_new.append(mh)
        return m_new, alpha

    def pv_update(c, p_scr, alpha):
        vc = vt_ref[0:V_GROUP, pl.ds(chunk_off(c), KEY_CHUNK)]
        for pair in range(SA_HEADS // 2):
            a2 = jnp.concatenate([alpha[2 * pair][:1], alpha[2 * pair + 1][:1]], axis=1)
            acc_scr[:, pair_cols(pair)] = a2 * acc_scr[:, pair_cols(pair)] + jnp.dot(
                vc, p_scr[:, pair_cols(pair)], preferred_element_type=_f32)

    def attn_body(j, carry):
        m, alpha_b = list(carry[:SA_HEADS]), list(carry[SA_HEADS:])
        pv_update(jnp.maximum(2 * j - 1, 0), pb_scr, alpha_b)
        m, alpha_a = softmax_chunk(2 * j, rawa_scr, pa_scr, m)
        score_dots(2 * j + 1, rawb_scr)
        pv_update(2 * j, pa_scr, alpha_a)
        m, alpha_b = softmax_chunk(2 * j + 1, rawb_scr, pb_scr, m)
        score_dots(jnp.minimum(2 * j + 2, last_chunk), rawa_scr)
        return tuple(m) + tuple(alpha_b)

    score_dots(0, rawa_scr)
    init = (tuple(jnp.full((SUBLANES, tq), NEG_INF, _f32) for _ in range(SA_HEADS))
            + tuple(jnp.zeros((SUBLANES, tq), _f32) for _ in range(SA_HEADS)))
    carry = lax.fori_loop(0, nsl, attn_body, init)
    pv_update(last_chunk, pb_scr, list(carry[SA_HEADS:]))

    def store_heads(o_ref, outs):
        for j in range(0, len(outs), 2):
            o_ref[:, j * HEAD_DIM:(j + 2) * HEAD_DIM] = jnp.concatenate(outs[j:j + 2], axis=0).T.astype(o_ref.dtype)

    acc = acc_scr[...]
    store_heads(osp_ref, [acc[0:HEAD_DIM, h * tq:(h + 1) * tq] / acc[HEAD_DIM:HEAD_DIM + 1, h * tq:(h + 1) * tq]
                          for h in range(SA_HEADS)])

    koff = pl.multiple_of(i * BLOCK, BLOCK)
    kidx = koff + krow
    diff = qcol - kidx
    wmask = (kidx >= N_PAD) & (diff >= 0) & (diff < WINDOW)
    outs = []
    for g in range(SW_KV_HEADS):
        for j in range(SW_GROUP):
            hq = g * SW_GROUP + j
            for half in range(SW_KV_HEADS):
                wq_scr[half * HEAD_DIM:(half + 1) * HEAD_DIM, j * tq:(j + 1) * tq] = (
                    qt_ref[QS_ROW + hq * HEAD_DIM:QS_ROW + (hq + 1) * HEAD_DIM, :] if half == g else zero_panel)
        s = jnp.dot(k_ref[pl.ds(koff, 2 * BLOCK), 2 * HEAD_DIM:4 * HEAD_DIM], wq_scr[:, 0:SW_GROUP * tq],
                    preferred_element_type=_f32)
        es, ms = [], []
        for j in range(SW_GROUP):
            sink = sink_ref[g * SW_GROUP + j] * LOG2E
            sh = jnp.where(wmask, s[:, j * tq:(j + 1) * tq], NEG_INF).reshape(n32, SUBLANES, tq)
            m = jnp.maximum(_sublane_all(_tree([sh[r] for r in range(n32)], jnp.maximum), jnp.maximum), sink)
            es.append(jnp.exp2(sh - m[None]).reshape(2 * BLOCK, tq).astype(_bf16))
            ms.append(jnp.exp2(sink - m[:1]))
        ev = jnp.dot(vt_ref[(1 + g) * V_GROUP:(2 + g) * V_GROUP, pl.ds(koff, 2 * BLOCK)],
                     jnp.concatenate(es, axis=1), preferred_element_type=_f32)
        for j in range(SW_GROUP):
            outs.append(ev[0:HEAD_DIM, j * tq:(j + 1) * tq]
                        / (ev[HEAD_DIM:HEAD_DIM + 1, j * tq:(j + 1) * tq] + ms[j]))
    store_heads(osw_ref, outs)


def _attention(sinks, qt, wit, k, vt, *, topk):
    B, _, S = qt.shape
    nkp = k.shape[1]
    tq = BLOCK
    kernel = functools.partial(_attn_kernel, topk=topk)
    out_blk = pl.BlockSpec((None, tq, SA_HEADS * HEAD_DIM), lambda b, i: (b, i, 0))
    return pl.pallas_call(
        kernel,
        out_shape=(jax.ShapeDtypeStruct((B, S, SA_HEADS * HEAD_DIM), _bf16),
                   jax.ShapeDtypeStruct((B, S, SW_HEADS * HEAD_DIM), _bf16)),
        grid=(B, S // tq),
        in_specs=[
            pl.BlockSpec(memory_space=pltpu.SMEM),
            pl.BlockSpec((None, QT_ROWS, tq), lambda b, i: (b, 0, i)),
            pl.BlockSpec((None, WI_ROWS, tq), lambda b, i: (b, 0, i)),
            pl.BlockSpec((None, nkp, K_COLS), lambda b, i: (b, 0, 0)),
            pl.BlockSpec((None, VT_ROWS, nkp), lambda b, i: (b, 0, 0)),
        ],
        out_specs=(out_blk, out_blk),
        scratch_shapes=[
            pltpu.VMEM((nkp, tq), _f32),
            pltpu.VMEM((2 * HEAD_DIM, SA_HEADS * tq), _bf16),
            pltpu.VMEM((KEY_CHUNK, SA_HEADS * tq), _f32),
            pltpu.VMEM((KEY_CHUNK, SA_HEADS * tq), _f32),
            pltpu.VMEM((KEY_CHUNK, SA_HEADS * tq), _bf16),
            pltpu.VMEM((KEY_CHUNK, SA_HEADS * tq), _bf16),
            pltpu.VMEM((V_GROUP, SA_HEADS * tq), _f32),
        ],
        compiler_params=pltpu.CompilerParams(
            dimension_semantics=("arbitrary", "arbitrary"),
            vmem_limit_bytes=VMEM_LIMIT_BYTES),
        name="attn",
    )(sinks, qt, wit, k, vt)


def _back_kernel(h1_ref, osp_ref, osw_ref, gate_ref, wbs_ref, wbw_ref, wo_ref, g2_ref,
                 wg_ref, wu_ref, wd_ref, gf_ref, out_ref):
    d = h1_ref.shape[-1]
    a = jnp.dot(osp_ref[...], wbs_ref[...], preferred_element_type=_f32)
    b = jnp.dot(osw_ref[...], wbw_ref[...], preferred_element_type=_f32)
    gates = gate_ref[...]
    merged = (gates[:, :d] * a + gates[:, d:] * b).astype(_bf16)
    h2 = h1_ref[...] + jnp.dot(merged, wo_ref[...], preferred_element_type=_f32)
    h3 = _swiglu_half_step(h2, g2_ref[...], wg_ref, wu_ref, wd_ref)
    out_ref[...] = _rms(h3, gf_ref[...])


def _back(h1, osp, osw, gates, wbs, wbw, wo, g2, wg, wu, wd, gf, *, tm):
    B, S, D = h1.shape
    F = wg.shape[1]
    row_blk = lambda w: pl.BlockSpec((None, tm, w), lambda b, j: (b, j, 0))
    return pl.pallas_call(
        _back_kernel,
        out_shape=jax.ShapeDtypeStruct((B, S, D), _f32),
        grid=(B, S // tm),
        in_specs=[
            row_blk(D), row_blk(osp.shape[-1]), row_blk(osw.shape[-1]), row_blk(gates.shape[-1]),
            _const_spec(wbs.shape), _const_spec(wbw.shape), _const_spec(wo.shape), _const_spec((1, D)),
            _const_spec((D, F)), _const_spec((D, F)), _const_spec((F, D)), _const_spec((1, D)),
        ],
        out_specs=row_blk(D),
        compiler_params=pltpu.CompilerParams(
            dimension_semantics=("arbitrary", "arbitrary"),
            vmem_limit_bytes=VMEM_LIMIT_BYTES),
        name="back",
    )(h1, osp, osw, gates, wbs, wbw, wo, g2, wg, wu, wd, gf)


def _rope_tables(rpos):
    inv_freq = 1.0 / (ROPE_THETA ** (jnp.arange(0, HEAD_DIM, 2, dtype=_f32) / HEAD_DIM))
    ang = rpos.astype(_f32)[:, None] * inv_freq[None, :]
    cos, sin = jnp.cos(ang), jnp.sin(ang)
    cos64 = jnp.concatenate([cos, cos], axis=1)
    sin64 = jnp.concatenate([-sin, sin], axis=1)
    reps = LANES // HEAD_DIM
    return (jnp.tile(cos64, (1, reps)), jnp.tile(sin64, (1, reps)), cos64.T, sin64.T)


def _pack_w_in(w):
    sizes = (SA_HEADS * HEAD_DIM, HEAD_DIM, HEAD_DIM, IDX_HEADS * IDX_DIM, IDX_DIM, IDX_HEADS,
             SW_HEADS * HEAD_DIM, SW_KV_HEADS * HEAD_DIM, SW_KV_HEADS * HEAD_DIM)
    d = w.shape[0]
    gate_cols = w.shape[1] - sum(sizes)
    splits = [int(s) for s in np.cumsum(sizes + (gate_cols,))[:-1]]
    qa, ka, va, qi, ki, wi, qs, ksw, vsw, gates = jnp.split(w, splits, axis=1)
    scale = HEAD_DIM ** -0.5
    idx_scale = (IDX_HEADS ** -0.5) * (IDX_DIM ** -0.5)
    zeros = lambda n: jnp.zeros((d, n), w.dtype)
    wrow = jnp.concatenate([ka, ki, ksw, gates], axis=1)
    pad = zeros(V_GROUP - HEAD_DIM)
    wt = jnp.concatenate(
        [qa * scale, qs * scale, qi, va, pad]
        + [blk for g in range(SW_KV_HEADS) for blk in (vsw[:, g * HEAD_DIM:(g + 1) * HEAD_DIM], pad)]
        + [wi * idx_scale, zeros(WI_ROWS - wi.shape[1])], axis=1).T
    return wrow.astype(_bf16), wt.astype(_bf16)


def kernel(x, meta_tokens, norm_ffn1, w_ffn1_gate, w_ffn1_up, w_ffn1_down, norm_mix, w_in, sinks,
           w_branch_sparse, w_branch_swa, w_out, norm_ffn2, w_ffn2_gate, w_ffn2_up, w_ffn2_down, norm_final):
    B, S, D = x.shape
    assert norm_ffn1.shape[0] == 1, "single-layer stack only"
    assert S % BLOCK == 0
    topk = min(TOPK_MAX, S // 4)
    assert topk <= KEY_CHUNK
    tm = 256 if S % 256 == 0 else BLOCK
    bf = lambda a: a.astype(_bf16)

    wrow, wt = _pack_w_in(w_in[0])
    g1, gm, g2 = norm_ffn1[0][None], norm_mix[0][None], norm_ffn2[0][None]
    wg1, wu1, wd1 = bf(w_ffn1_gate[0]), bf(w_ffn1_up[0]), bf(w_ffn1_down[0])

    pos_real = N_META + jnp.arange(S, dtype=jnp.int32)
    pos_meta = jnp.maximum(jnp.arange(BLOCK, dtype=jnp.int32) - N_PAD, 0)
    h1, k, gates, qt, vt, wit = _front(x, _rope_tables(pos_real), g1, gm, wg1, wu1, wd1, wrow, wt, tm=tm)
    meta_blk = jnp.concatenate([jnp.zeros((N_PAD, D), x.dtype), meta_tokens.astype(x.dtype)], axis=0)[None]
    _, k_m, _, _, vt_m, _ = _front(meta_blk, _rope_tables(pos_meta), g1, gm, wg1, wu1, wd1, wrow, wt, tm=BLOCK)

    nkp = -(-(S + BLOCK) // SEARCH_CHUNK) * SEARCH_CHUNK
    k_all = jnp.concatenate([jnp.broadcast_to(k_m, (B, BLOCK, K_COLS)), k,
                             jnp.zeros((B, nkp - S - BLOCK, K_COLS), k.dtype)], axis=1)
    vt_all = jnp.concatenate([jnp.broadcast_to(vt_m, (B, VT_ROWS, BLOCK)), vt,
                              jnp.zeros((B, VT_ROWS, nkp - S - BLOCK), vt.dtype)], axis=2)

    o_sparse, o_swa = _attention(sinks[0].astype(_f32), qt, wit, k_all, vt_all, topk=topk)

    return _back(h1, o_sparse, o_swa, gates, bf(w_branch_sparse[0]), bf(w_branch_swa[0]), bf(w_out[0]), g2,
                 bf(w_ffn2_gate[0]), bf(w_ffn2_up[0]), bf(w_ffn2_down[0]), norm_final[None], tm=tm)
```

```python
import functools

import numpy as np
import jax
import jax.numpy as jnp
from jax import lax
from jax.experimental import pallas as pl
from jax.experimental.pallas import tpu as pltpu

N_META = 16
BLOCK = 128
HEAD_DIM = 64
ROPE_THETA = 10000.0
EPS = 1e-6
NEG_INF = -1e30
SA_HEADS = 8
TOPK_MAX = 256
IDX_HEADS = 4
IDX_DIM = 64
SW_HEADS = 8
SW_KV_HEADS = 2
WINDOW = 128
N_PAD = BLOCK - N_META
SW_GROUP = SW_HEADS // SW_KV_HEADS
LOG2E = 1.4426950408889634

LANES = 128
SUBLANES = 8
PACK16 = 16
KEY_CHUNK = 256
SEARCH_CHUNK = 2 * KEY_CHUNK
VMEM_LIMIT_BYTES = 56 * 1024 * 1024

QT_ROWS = (SA_HEADS + SW_HEADS + IDX_HEADS) * HEAD_DIM
QS_ROW = SA_HEADS * HEAD_DIM
QI_ROW = QS_ROW + SW_HEADS * HEAD_DIM
V_GROUP = HEAD_DIM + PACK16
VT_ROWS = (1 + SW_KV_HEADS) * V_GROUP
WI_ROWS = PACK16
PT_ROWS = QT_ROWS + VT_ROWS + WI_ROWS
K_COLS = 4 * HEAD_DIM
SEL_FLOOR = float(np.nextafter(np.float32(0.5 * NEG_INF), np.float32(0.0)))
KEY_BELOW_ALL = int(np.int32(np.uint32(0xFF800000) ^ np.uint32(0x7FFFFFFF)))
KEY_ABOVE_ALL = 0x7F800001
VALUE_BISECT_STEPS = 8
BISECT_STEPS_PER_CHECK = 4

_f32 = jnp.float32
_bf16 = jnp.bfloat16
_i32 = jnp.int32


def _rms(x, g):
    return x * lax.rsqrt(jnp.mean(x * x, axis=-1, keepdims=True) + EPS) * g


def _swiglu_half_step(x, g_norm, wg_ref, wu_ref, wd_ref):
    a = _rms(x, g_norm).astype(_bf16)
    g = jnp.dot(a, wg_ref[...], preferred_element_type=_f32)
    u = jnp.dot(a, wu_ref[...], preferred_element_type=_f32)
    act = (g * jax.nn.sigmoid(g) * u).astype(_bf16)
    return x + 0.5 * jnp.dot(act, wd_ref[...], preferred_element_type=_f32)


def _front_kernel(x_ref, cos_ref, sin_ref, cost_ref, sint_ref, g1_ref, gm_ref,
                  wg_ref, wu_ref, wd_ref, wrow_ref, wt_ref,
                  h1_ref, k_ref, gate_ref, qt_ref, vt_ref, wit_ref):
    h1 = _swiglu_half_step(x_ref[...], g1_ref[...], wg_ref, wu_ref, wd_ref)
    h1_ref[...] = h1
    u = _rms(h1, gm_ref[...]).astype(_bf16)
    half = HEAD_DIM // 2

    p = jnp.dot(u, wrow_ref[...], preferred_element_type=_f32)
    pk = p[:, :K_COLS]
    lane = lax.broadcasted_iota(_i32, pk.shape, 1)
    swapped = jnp.where((lane & (HEAD_DIM - 1)) < half,
                        pltpu.roll(pk, K_COLS - half, 1), pltpu.roll(pk, half, 1))
    reps = K_COLS // LANES
    cos = jnp.concatenate([cos_ref[...]] * reps, axis=1)
    sin = jnp.concatenate([sin_ref[...]] * reps, axis=1)
    k_ref[...] = (pk * cos + swapped * sin).astype(_bf16)
    gate_ref[...] = jax.nn.sigmoid(p[:, K_COLS:])

    pt = lax.dot_general(wt_ref[...], u, (((1,), (1,)), ((), ())),
                         preferred_element_type=_f32)
    tm = pt.shape[1]
    nq = QT_ROWS // HEAD_DIM
    q3 = pt[:QT_ROWS].reshape(nq, HEAD_DIM, tm)
    sw = jnp.concatenate([q3[:, half:], q3[:, :half]], axis=1)
    q3 = q3 * cost_ref[...][None] + sw * sint_ref[...][None]
    head = lax.broadcasted_iota(_i32, (nq, 1, 1), 0)
    q3 = q3 * jnp.where(head < SA_HEADS + SW_HEADS, LOG2E, 1.0)
    qt_ref[...] = q3.reshape(QT_ROWS, tm).astype(_bf16)
    vrow = lax.broadcasted_iota(_i32, (VT_ROWS, tm), 0)
    ones_row = jnp.zeros((VT_ROWS, tm), _f32)
    for g in range(VT_ROWS // V_GROUP):
        ones_row = jnp.where(vrow == g * V_GROUP + HEAD_DIM, 1.0, ones_row)
    vt_ref[...] = (pt[QT_ROWS:QT_ROWS + VT_ROWS] + ones_row).astype(_bf16)
    wit_ref[...] = pt[QT_ROWS + VT_ROWS:]


def _const_spec(shape):
    nd = len(shape)
    return pl.BlockSpec(shape, lambda *_: (0,) * nd, pipeline_mode=pl.Buffered(1))


def _front(xs, tabs, g1, gm, wg, wu, wd, wrow, wt, *, tm):
    B, S, D = xs.shape
    F = wg.shape[1]
    row_cos, row_sin, col_cos, col_sin = tabs
    gate_cols = wrow.shape[1] - K_COLS
    row_blk = lambda w: pl.BlockSpec((None, tm, w), lambda b, j: (b, j, 0))
    col_blk = lambda r: pl.BlockSpec((None, r, tm), lambda b, j: (b, 0, j))
    out_shape = (
        jax.ShapeDtypeStruct((B, S, D), _f32),
        jax.ShapeDtypeStruct((B, S, K_COLS), _bf16),
        jax.ShapeDtypeStruct((B, S, gate_cols), _f32),
        jax.ShapeDtypeStruct((B, QT_ROWS, S), _bf16),
        jax.ShapeDtypeStruct((B, VT_ROWS, S), _bf16),
        jax.ShapeDtypeStruct((B, WI_ROWS, S), _f32),
    )
    return pl.pallas_call(
        _front_kernel,
        out_shape=out_shape,
        grid=(B, S // tm),
        in_specs=[
            row_blk(D),
            pl.BlockSpec((tm, LANES), lambda b, j: (j, 0)),
            pl.BlockSpec((tm, LANES), lambda b, j: (j, 0)),
            pl.BlockSpec((HEAD_DIM, tm), lambda b, j: (0, j)),
            pl.BlockSpec((HEAD_DIM, tm), lambda b, j: (0, j)),
            _const_spec((1, D)), _const_spec((1, D)),
            _const_spec((D, F)), _const_spec((D, F)), _const_spec((F, D)),
            _const_spec(wrow.shape), _const_spec(wt.shape),
        ],
        out_specs=(row_blk(D), row_blk(K_COLS), row_blk(gate_cols),
                   col_blk(QT_ROWS), col_blk(VT_ROWS), col_blk(WI_ROWS)),
        compiler_params=pltpu.CompilerParams(
            dimension_semantics=("arbitrary", "arbitrary"),
            vmem_limit_bytes=VMEM_LIMIT_BYTES),
        name="front",
    )(xs, row_cos, row_sin, col_cos, col_sin, g1, gm, wg, wu, wd, wrow, wt)


def _sortable(bits):
    return bits ^ (lax.shift_right_arithmetic(bits, 31) & jnp.int32(0x7FFFFFFF))


def _key_to_float(key):
    return lax.bitcast_convert_type(_sortable(key), _f32)


def _float_to_key(x):
    return _sortable(lax.bitcast_convert_type(x, _i32))


def _sublane_all(x, op):
    x = op(x, pltpu.roll(x, 4, 0))
    x = op(x, pltpu.roll(x, 2, 0))
    return op(x, pltpu.roll(x, 1, 0))


def _tree(parts, op):
    while len(parts) > 1:
        parts = [op(parts[j], parts[j + 1]) if j + 1 < len(parts) else parts[j] for j in range(0, len(parts), 2)]
    return parts[0]


def _attn_kernel(sink_ref, qt_ref, wit_ref, k_ref, vt_ref, osp_ref, osw_ref,
                 st_scr, tie_scr, wq_scr, rawa_scr, rawb_scr, pa_scr, pb_scr, acc_scr, *, topk):
    tq = BLOCK
    i = pl.program_id(1)
    nch = (i + 3) // 2
    nsl = (nch + 1) // 2
    n32 = KEY_CHUNK // SUBLANES
    s32 = SEARCH_CHUNK // SUBLANES
    krow = lax.broadcasted_iota(_i32, (KEY_CHUNK, tq), 0)
    srow = lax.broadcasted_iota(_i32, (SEARCH_CHUNK, tq), 0)
    qcol = BLOCK * (i + 1) + lax.broadcasted_iota(_i32, (KEY_CHUNK, tq), 1)

    def chunk_off(c):
        return pl.multiple_of(c * KEY_CHUNK, KEY_CHUNK)

    def slab_off(c):
        return pl.multiple_of(c * SEARCH_CHUNK, SEARCH_CHUNK)

    def pair_cols(pair):
        return slice(2 * pair * tq, 2 * (pair + 1) * tq)


    zero_panel = jnp.zeros((HEAD_DIM, tq), _bf16)
    for h in range(IDX_HEADS):
        wq_scr[0:HEAD_DIM, h * tq:(h + 1) * tq] = zero_panel
        wq_scr[HEAD_DIM:, h * tq:(h + 1) * tq] = qt_ref[QI_ROW + h * HEAD_DIM:QI_ROW + (h + 1) * HEAD_DIM, :]
    wi = wit_ref[...]
    wrow = jnp.concatenate([jnp.broadcast_to(wi[h:h + 1], (SUBLANES, tq)) for h in range(IDX_HEADS)], axis=1)

    def index_dots(c, raw_scr):
        kc = k_ref[pl.ds(chunk_off(c), KEY_CHUNK), 0:2 * HEAD_DIM]
        for pair in range(IDX_HEADS // 2):
            raw_scr[:, pair_cols(pair)] = jnp.dot(kc, wq_scr[:, pair_cols(pair)], preferred_element_type=_f32)

    def index_finish(c, raw_scr):
        off = chunk_off(c)
        terms = []
        for pair in range(IDX_HEADS // 2):
            d = jnp.maximum(raw_scr[:, pair_cols(pair)], 0.0).reshape(n32, SUBLANES, 2 * tq)
            d = d * wrow[None, :, pair_cols(pair)]
            terms += [d[:, :, :tq], d[:, :, tq:]]
        sc = _tree(terms, jnp.add).reshape(KEY_CHUNK, tq)
        kidx = off + krow
        st_scr[pl.ds(off, KEY_CHUNK), :] = jnp.where((kidx >= N_PAD) & (kidx <= qcol), sc, NEG_INF)

    last_chunk = 2 * nsl - 1

    def index_body(j, carry):
        index_dots(2 * j + 1, rawb_scr)
        index_finish(2 * j, rawa_scr)
        index_dots(jnp.minimum(2 * j + 2, last_chunk), rawa_scr)
        index_finish(2 * j + 1, rawb_scr)
        return carry

    index_dots(0, rawa_scr)
    lax.fori_loop(0, nsl, index_body, 0)

    def count32(pred):
        def body(c, acc):
            off = slab_off(c)
            st = st_scr[pl.ds(off, SEARCH_CHUNK), :].reshape(s32 // 4, 4, SUBLANES, tq)
            kidx = off + srow.reshape(s32 // 4, 4, SUBLANES, tq)
            ones = jnp.where(pred(st, kidx), 1, 0).astype(_i32)
            return acc + _tree([ones[j] for j in range(s32 // 4)], jnp.add)
        acc = lax.fori_loop(0, nsl, body, jnp.zeros((4, SUBLANES, tq), _i32))
        return _sublane_all(_tree([acc[j] for j in range(4)], jnp.add), jnp.add)

    def count_ge(key):
        t = _key_to_float(key)
        return count32(lambda st, kidx: st >= t)

    def bound_body(c, acc):
        return jnp.maximum(acc, st_scr[pl.ds(chunk_off(c), KEY_CHUNK), :].reshape(n32, SUBLANES, tq))
    tops = lax.fori_loop(0, 2 * nsl, bound_body, jnp.full((n32, SUBLANES, tq), NEG_INF, _f32))
    ub = _sublane_all(_tree([tops[j] for j in range(n32)], jnp.maximum), jnp.maximum)
    lb = _sublane_all(_tree([tops[j] for j in range(n32)], jnp.minimum), jnp.minimum)

    def probe(lo, hi, mid):
        mid = jnp.minimum(jnp.maximum(mid, lo), hi - 1)
        cnt = count_ge(mid)
        ge = cnt >= topk
        return jnp.where(ge, mid, lo), jnp.where(cnt == topk, mid + 1, jnp.where(ge, hi, mid))

    def key_mid(lo, hi):
        return lax.shift_right_arithmetic(lo, 1) + lax.shift_right_arithmetic(hi, 1) + (lo & hi & 1)

    def value_mid(lo, hi):
        mid = _float_to_key(0.5 * (_key_to_float(lo) + _key_to_float(hi)))
        return jnp.where((mid > lo) & (mid < hi), mid, key_mid(lo, hi))

    c_ge0 = count32(lambda st, kidx: st >= 0.0)
    c_gt0 = count32(lambda st, kidx: st > 0.0)
    at_zero = (c_gt0 < topk) & (c_ge0 >= topk)
    lo = jnp.where(c_ge0 >= topk, 0, KEY_BELOW_ALL)
    hi = jnp.where(at_zero, 1, jnp.where(c_ge0 >= topk, KEY_ABOVE_ALL, 0))
    lo, hi = probe(lo, hi, _float_to_key(lb))
    lo, hi = probe(lo, hi, _float_to_key(ub) + 1)
    for _ in range(VALUE_BISECT_STEPS):
        lo, hi = probe(lo, hi, value_mid(lo, hi))

    def search_body(state):
        lo, hi, _ = state
        for _ in range(BISECT_STEPS_PER_CHECK):
            lo, hi = probe(lo, hi, key_mid(lo, hi))
        return lo, hi, jnp.max(jnp.where(hi - lo != 1, 1, 0).astype(_i32))

    pending = jnp.max(jnp.where(hi - lo != 1, 1, 0).astype(_i32))
    lo, hi, _ = lax.while_loop(lambda state: state[2] > 0, search_body, (lo, hi, pending))
    t8 = _key_to_float(lo)

    cnt_gt = count32(lambda st, kidx: st > t8)
    cnt_ge = count32(lambda st, kidx: st >= t8)
    need = topk - cnt_gt
    tie_rows = (cnt_ge - cnt_gt > need) & (t8 >= SEL_FLOOR)
    has_ties = jnp.max(jnp.where(tie_rows, 1, 0).astype(_i32))

    @pl.when(has_ties > 0)
    def _():
        def mark_body(c, carry):
            off = slab_off(c)
            st = st_scr[pl.ds(off, SEARCH_CHUNK), :].reshape(s32, SUBLANES, tq)
            kidx = off + srow.reshape(s32, SUBLANES, tq)
            tie_scr[pl.ds(off, SEARCH_CHUNK), :] = jnp.where(
                st == t8[None], kidx, jnp.int32(2 ** 30)).reshape(SEARCH_CHUNK, tq)
            return carry
        lax.fori_loop(0, nsl, mark_body, 0)

        def count_below(cand):
            def body(c, acc):
                x = tie_scr[pl.ds(slab_off(c), SEARCH_CHUNK), :].reshape(s32 // 4, 4, SUBLANES, tq)
                ones = jnp.where(x < cand, 1, 0).astype(_i32)
                return acc + _tree([ones[j] for j in range(s32 // 4)], jnp.add)
            acc = lax.fori_loop(0, nsl, body, jnp.zeros((4, SUBLANES, tq), _i32))
            return _sublane_all(_tree([acc[j] for j in range(4)], jnp.add), jnp.add)

        def jbit_body(it, j8):
            cand = j8 | lax.shift_left(jnp.int32(1), 13 - it)
            return jnp.where(count_below(cand) <= need, cand, j8)
        j8 = lax.fori_loop(0, 14, jbit_body, jnp.zeros((SUBLANES, tq), _i32))

        def fix_body(c, carry):
            off = slab_off(c)
            st = st_scr[pl.ds(off, SEARCH_CHUNK), :].reshape(s32, SUBLANES, tq)
            x = tie_scr[pl.ds(off, SEARCH_CHUNK), :].reshape(s32, SUBLANES, tq)
            st_scr[pl.ds(off, SEARCH_CHUNK), :] = jnp.where(
                (x >= j8[None]) & (x < 2 ** 30), NEG_INF, st).reshape(SEARCH_CHUNK, tq)
            return carry
        lax.fori_loop(0, nsl, fix_body, 0)

    tsel = jnp.maximum(t8, SEL_FLOOR)

    for h in range(SA_HEADS):
        wq_scr[0:HEAD_DIM, h * tq:(h + 1) * tq] = qt_ref[h * HEAD_DIM:(h + 1) * HEAD_DIM, :]
        wq_scr[HEAD_DIM:, h * tq:(h + 1) * tq] = zero_panel
    pb_scr[...] = jnp.zeros(pb_scr.shape, _bf16)
    acc_scr[...] = jnp.zeros(acc_scr.shape, _f32)

    def score_dots(c, raw_scr):
        kc = k_ref[pl.ds(chunk_off(c), KEY_CHUNK), 0:2 * HEAD_DIM]
        for pair in range(SA_HEADS // 2):
            raw_scr[:, pair_cols(pair)] = jnp.dot(kc, wq_scr[:, pair_cols(pair)], preferred_element_type=_f32)

    def softmax_chunk(c, raw_scr, p_scr, m_old):
        sel = st_scr[pl.ds(chunk_off(c), KEY_CHUNK), :].reshape(n32, SUBLANES, tq) >= tsel[None]
        m_new, alpha = [], []
        for h in range(SA_HEADS):
            sh = jnp.where(sel, raw_scr[:, h * tq:(h + 1) * tq].reshape(n32, SUBLANES, tq), NEG_INF)
            mh = jnp.maximum(m_old[h], _sublane_all(_tree([sh[j] for j in range(n32)], jnp.maximum), jnp.maximum))
            p_scr[:, h * tq:(h + 1) * tq] = jnp.exp2(sh - mh[None]).reshape(KEY_CHUNK, tq).astype(_bf16)
            alpha.append(jnp.exp2(m_old[h] - mh))
            m_new.append(mh)
        return m_new, alpha

    def pv_update(c, p_scr, alpha):
        vc = vt_ref[0:V_GROUP, pl.ds(chunk_off(c), KEY_CHUNK)]
        for pair in range(SA_HEADS // 2):
            a2 = jnp.concatenate([alpha[2 * pair][:1], alpha[2 * pair + 1][:1]], axis=1)
            acc_scr[:, pair_cols(pair)] = a2 * acc_scr[:, pair_cols(pair)] + jnp.dot(
                vc, p_scr[:, pair_cols(pair)], preferred_element_type=_f32)

    def attn_body(j, carry):
        m, alpha_b = list(carry[:SA_HEADS]), list(carry[SA_HEADS:])
        pv_update(jnp.maximum(2 * j - 1, 0), pb_scr, alpha_b)
        m, alpha_a = softmax_chunk(2 * j, rawa_scr, pa_scr, m)
        score_dots(2 * j + 1, rawb_scr)
        pv_update(2 * j, pa_scr, alpha_a)
        m, alpha_b = softmax_chunk(2 * j + 1, rawb_scr, pb_scr, m)
        score_dots(jnp.minimum(2 * j + 2, last_chunk), rawa_scr)
        return tuple(m) + tuple(alpha_b)

    score_dots(0, rawa_scr)
    init = (tuple(jnp.full((SUBLANES, tq), NEG_INF, _f32) for _ in range(SA_HEADS))
            + tuple(jnp.zeros((SUBLANES, tq), _f32) for _ in range(SA_HEADS)))
    carry = lax.fori_loop(0, nsl, attn_body, init)
    pv_update(last_chunk, pb_scr, list(carry[SA_HEADS:]))

    def store_heads(o_ref, outs):
        for j in range(0, len(outs), 2):
            o_ref[:, j * HEAD_DIM:(j + 2) * HEAD_DIM] = jnp.concatenate(outs[j:j + 2], axis=0).T.astype(o_ref.dtype)

    acc = acc_scr[...]
    store_heads(osp_ref, [acc[0:HEAD_DIM, h * tq:(h + 1) * tq] / acc[HEAD_DIM:HEAD_DIM + 1, h * tq:(h + 1) * tq]
                          for h in range(SA_HEADS)])

    koff = pl.multiple_of(i * BLOCK, BLOCK)
    kidx = koff + krow
    diff = qcol - kidx
    wmask = (kidx >= N_PAD) & (diff >= 0) & (diff < WINDOW)
    outs = []
    for g in range(SW_KV_HEADS):
        for j in range(SW_GROUP):
            hq = g * SW_GROUP + j
            for half in range(SW_KV_HEADS):
                wq_scr[half * HEAD_DIM:(half + 1) * HEAD_DIM, j * tq:(j + 1) * tq] = (
                    qt_ref[QS_ROW + hq * HEAD_DIM:QS_ROW + (hq + 1) * HEAD_DIM, :] if half == g else zero_panel)
        s = jnp.dot(k_ref[pl.ds(koff, 2 * BLOCK), 2 * HEAD_DIM:4 * HEAD_DIM], wq_scr[:, 0:SW_GROUP * tq],
                    preferred_element_type=_f32)
        es, ms = [], []
        for j in range(SW_GROUP):
            sink = sink_ref[g * SW_GROUP + j] * LOG2E
            sh = jnp.where(wmask, s[:, j * tq:(j + 1) * tq], NEG_INF).reshape(n32, SUBLANES, tq)
            m = jnp.maximum(_sublane_all(_tree([sh[r] for r in range(n32)], jnp.maximum), jnp.maximum), sink)
            es.append(jnp.exp2(sh - m[None]).reshape(2 * BLOCK, tq).astype(_bf16))
            ms.append(jnp.exp2(sink - m[:1]))
        ev = jnp.dot(vt_ref[(1 + g) * V_GROUP:(2 + g) * V_GROUP, pl.ds(koff, 2 * BLOCK)],
                     jnp.concatenate(es, axis=1), preferred_element_type=_f32)
        for j in range(SW_GROUP):
            outs.append(ev[0:HEAD_DIM, j * tq:(j + 1) * tq]
                        / (ev[HEAD_DIM:HEAD_DIM + 1, j * tq:(j + 1) * tq] + ms[j]))
    store_heads(osw_ref, outs)


def _attention(sinks, qt, wit, k, vt, *, topk):
    B, _, S = qt.shape
    nkp = k.shape[1]
    tq = BLOCK
    kernel = functools.partial(_attn_kernel, topk=topk)
    out_blk = pl.BlockSpec((None, tq, SA_HEADS * HEAD_DIM), lambda b, i: (b, i, 0))
    return pl.pallas_call(
        kernel,
        out_shape=(jax.ShapeDtypeStruct((B, S, SA_HEADS * HEAD_DIM), _bf16),
                   jax.ShapeDtypeStruct((B, S, SW_HEADS * HEAD_DIM), _bf16)),
        grid=(B, S // tq),
        in_specs=[
            pl.BlockSpec(memory_space=pltpu.SMEM),
            pl.BlockSpec((None, QT_ROWS, tq), lambda b, i: (b, 0, i)),
            pl.BlockSpec((None, WI_ROWS, tq), lambda b, i: (b, 0, i)),
            pl.BlockSpec((None, nkp, K_COLS), lambda b, i: (b, 0, 0)),
            pl.BlockSpec((None, VT_ROWS, nkp), lambda b, i: (b, 0, 0)),
        ],
        out_specs=(out_blk, out_blk),
        scratch_shapes=[
            pltpu.VMEM((nkp, tq), _f32),
            pltpu.VMEM((nkp, tq), _i32),
            pltpu.VMEM((2 * HEAD_DIM, SA_HEADS * tq), _bf16),
            pltpu.VMEM((KEY_CHUNK, SA_HEADS * tq), _f32),
            pltpu.VMEM((KEY_CHUNK, SA_HEADS * tq), _f32),
            pltpu.VMEM((KEY_CHUNK, SA_HEADS * tq), _bf16),
            pltpu.VMEM((KEY_CHUNK, SA_HEADS * tq), _bf16),
            pltpu.VMEM((V_GROUP, SA_HEADS * tq), _f32),
        ],
        compiler_params=pltpu.CompilerParams(
            dimension_semantics=("arbitrary", "arbitrary"),
            vmem_limit_bytes=VMEM_LIMIT_BYTES),
        name="attn",
    )(sinks, qt, wit, k, vt)


def _back_kernel(h1_ref, osp_ref, osw_ref, gate_ref, wbs_ref, wbw_ref, wo_ref, g2_ref,
                 wg_ref, wu_ref, wd_ref, gf_ref, out_ref):
    d = h1_ref.shape[-1]
    a = jnp.dot(osp_ref[...], wbs_ref[...], preferred_element_type=_f32)
    b = jnp.dot(osw_ref[...], wbw_ref[...], preferred_element_type=_f32)
    gates = gate_ref[...]
    merged = (gates[:, :d] * a + gates[:, d:] * b).astype(_bf16)
    h2 = h1_ref[...] + jnp.dot(merged, wo_ref[...], preferred_element_type=_f32)
    h3 = _swiglu_half_step(h2, g2_ref[...], wg_ref, wu_ref, wd_ref)
    out_ref[...] = _rms(h3, gf_ref[...])


def _back(h1, osp, osw, gates, wbs, wbw, wo, g2, wg, wu, wd, gf, *, tm):
    B, S, D = h1.shape
    F = wg.shape[1]
    row_blk = lambda w: pl.BlockSpec((None, tm, w), lambda b, j: (b, j, 0))
    return pl.pallas_call(
        _back_kernel,
        out_shape=jax.ShapeDtypeStruct((B, S, D), _f32),
        grid=(B, S // tm),
        in_specs=[
            row_blk(D), row_blk(osp.shape[-1]), row_blk(osw.shape[-1]), row_blk(gates.shape[-1]),
            _const_spec(wbs.shape), _const_spec(wbw.shape), _const_spec(wo.shape), _const_spec((1, D)),
            _const_spec((D, F)), _const_spec((D, F)), _const_spec((F, D)), _const_spec((1, D)),
        ],
        out_specs=row_blk(D),
        compiler_params=pltpu.CompilerParams(
            dimension_semantics=("arbitrary", "arbitrary"),
            vmem_limit_bytes=VMEM_LIMIT_BYTES),
        name="back",
    )(h1, osp, osw, gates, wbs, wbw, wo, g2, wg, wu, wd, gf)


def _rope_tables(rpos):
    inv_freq = 1.0 / (ROPE_THETA ** (jnp.arange(0, HEAD_DIM, 2, dtype=_f32) / HEAD_DIM))
    ang = rpos.astype(_f32)[:, None] * inv_freq[None, :]
    cos, sin = jnp.cos(ang), jnp.sin(ang)
    cos64 = jnp.concatenate([cos, cos], axis=1)
    sin64 = jnp.concatenate([-sin, sin], axis=1)
    reps = LANES // HEAD_DIM
    return (jnp.tile(cos64, (1, reps)), jnp.tile(sin64, (1, reps)), cos64.T, sin64.T)


def _pack_w_in(w):
    sizes = (SA_HEADS * HEAD_DIM, HEAD_DIM, HEAD_DIM, IDX_HEADS * IDX_DIM, IDX_DIM, IDX_HEADS,
             SW_HEADS * HEAD_DIM, SW_KV_HEADS * HEAD_DIM, SW_KV_HEADS * HEAD_DIM)
    d = w.shape[0]
    gate_cols = w.shape[1] - sum(sizes)
    splits = [int(s) for s in np.cumsum(sizes + (gate_cols,))[:-1]]
    qa, ka, va, qi, ki, wi, qs, ksw, vsw, gates = jnp.split(w, splits, axis=1)
    scale = HEAD_DIM ** -0.5
    idx_scale = (IDX_HEADS ** -0.5) * (IDX_DIM ** -0.5)
    zeros = lambda n: jnp.zeros((d, n), w.dtype)
    wrow = jnp.concatenate([ka, ki, ksw, gates], axis=1)
    pad = zeros(V_GROUP - HEAD_DIM)
    wt = jnp.concatenate(
        [qa * scale, qs * scale, qi, va, pad]
        + [blk for g in range(SW_KV_HEADS) for blk in (vsw[:, g * HEAD_DIM:(g + 1) * HEAD_DIM], pad)]
        + [wi * idx_scale, zeros(WI_ROWS - wi.shape[1])], axis=1).T
    return wrow.astype(_bf16), wt.astype(_bf16)


def kernel(x, meta_tokens, norm_ffn1, w_ffn1_gate, w_ffn1_up, w_ffn1_down, norm_mix, w_in, sinks,
           w_branch_sparse, w_branch_swa, w_out, norm_ffn2, w_ffn2_gate, w_ffn2_up, w_ffn2_down, norm_final):
    B, S, D = x.shape
    assert norm_ffn1.shape[0] == 1, "single-layer stack only"
    assert S % BLOCK == 0
    topk = min(TOPK_MAX, S // 4)
    assert topk <= KEY_CHUNK
    tm = 256 if S % 256 == 0 else BLOCK
    bf = lambda a: a.astype(_bf16)

    wrow, wt = _pack_w_in(w_in[0])
    g1, gm, g2 = norm_ffn1[0][None], norm_mix[0][None], norm_ffn2[0][None]
    wg1, wu1, wd1 = bf(w_ffn1_gate[0]), bf(w_ffn1_up[0]), bf(w_ffn1_down[0])

    pos_real = N_META + jnp.arange(S, dtype=jnp.int32)
    pos_meta = jnp.maximum(jnp.arange(BLOCK, dtype=jnp.int32) - N_PAD, 0)
    h1, k, gates, qt, vt, wit = _front(x, _rope_tables(pos_real), g1, gm, wg1, wu1, wd1, wrow, wt, tm=tm)
    meta_blk = jnp.concatenate([jnp.zeros((N_PAD, D), x.dtype), meta_tokens.astype(x.dtype)], axis=0)[None]
    _, k_m, _, _, vt_m, _ = _front(meta_blk, _rope_tables(pos_meta), g1, gm, wg1, wu1, wd1, wrow, wt, tm=BLOCK)

    nkp = -(-(S + BLOCK) // SEARCH_CHUNK) * SEARCH_CHUNK
    k_all = jnp.concatenate([jnp.broadcast_to(k_m, (B, BLOCK, K_COLS)), k,
                             jnp.zeros((B, nkp - S - BLOCK, K_COLS), k.dtype)], axis=1)
    vt_all = jnp.concatenate([jnp.broadcast_to(vt_m, (B, VT_ROWS, BLOCK)), vt,
                              jnp.zeros((B, VT_ROWS, nkp - S - BLOCK), vt.dtype)], axis=2)

    o_sparse, o_swa = _attention(sinks[0].astype(_f32), qt, wit, k_all, vt_all, topk=topk)

    return _back(h1, o_sparse, o_swa, gates, bf(w_branch_sparse[0]), bf(w_branch_swa[0]), bf(w_out[0]), g2,
                 bf(w_ffn2_gate[0]), bf(w_ffn2_up[0]), bf(w_ffn2_down[0]), norm_final[None], tm=tm)
```

```python
import functools

import numpy as np
import jax
import jax.numpy as jnp
from jax import lax
from jax.experimental import pallas as pl
from jax.experimental.pallas import tpu as pltpu

N_META = 16
BLOCK = 128
HEAD_DIM = 64
ROPE_THETA = 10000.0
EPS = 1e-6
NEG_INF = -1e30
SA_HEADS = 8
TOPK_MAX = 256
IDX_HEADS = 4
IDX_DIM = 64
SW_HEADS = 8
SW_KV_HEADS = 2
WINDOW = 128
N_PAD = BLOCK - N_META
SW_GROUP = SW_HEADS // SW_KV_HEADS
LOG2E = 1.4426950408889634

LANES = 128
SUBLANES = 8
PACK16 = 16
KEY_CHUNK = 256
SEARCH_CHUNK = 2 * KEY_CHUNK
VMEM_LIMIT_BYTES = 56 * 1024 * 1024

QT_ROWS = (SA_HEADS + SW_HEADS + IDX_HEADS) * HEAD_DIM
QS_ROW = SA_HEADS * HEAD_DIM
QI_ROW = QS_ROW + SW_HEADS * HEAD_DIM
V_GROUP = HEAD_DIM + PACK16
VT_ROWS = (1 + SW_KV_HEADS) * V_GROUP
WI_ROWS = PACK16
PT_ROWS = QT_ROWS + VT_ROWS + WI_ROWS
K_COLS = 4 * HEAD_DIM
SEL_FLOOR = float(np.nextafter(np.float32(0.5 * NEG_INF), np.float32(0.0)))
KEY_BELOW_ALL = int(np.int32(np.uint32(0xFF800000) ^ np.uint32(0x7FFFFFFF)))
KEY_ABOVE_ALL = 0x7F800001
VALUE_BISECT_STEPS = 12
BISECT_STEPS_PER_CHECK = 3

_f32 = jnp.float32
_bf16 = jnp.bfloat16
_i32 = jnp.int32


def _rms(x, g):
    return x * lax.rsqrt(jnp.mean(x * x, axis=-1, keepdims=True) + EPS) * g


def _swiglu_half_step(x, g_norm, wg_ref, wu_ref, wd_ref):
    a = _rms(x, g_norm).astype(_bf16)
    g = jnp.dot(a, wg_ref[...], preferred_element_type=_f32)
    u = jnp.dot(a, wu_ref[...], preferred_element_type=_f32)
    act = (g * jax.nn.sigmoid(g) * u).astype(_bf16)
    return x + 0.5 * jnp.dot(act, wd_ref[...], preferred_element_type=_f32)


def _front_kernel(x_ref, cos_ref, sin_ref, cost_ref, sint_ref, g1_ref, gm_ref,
                  wg_ref, wu_ref, wd_ref, wrow_ref, wt_ref,
                  h1_ref, k_ref, gate_ref, qt_ref, vt_ref, wit_ref):
    h1 = _swiglu_half_step(x_ref[...], g1_ref[...], wg_ref, wu_ref, wd_ref)
    h1_ref[...] = h1
    u = _rms(h1, gm_ref[...]).astype(_bf16)
    half = HEAD_DIM // 2

    p = jnp.dot(u, wrow_ref[...], preferred_element_type=_f32)
    pk = p[:, :K_COLS]
    lane = lax.broadcasted_iota(_i32, pk.shape, 1)
    swapped = jnp.where((lane & (HEAD_DIM - 1)) < half,
                        pltpu.roll(pk, K_COLS - half, 1), pltpu.roll(pk, half, 1))
    reps = K_COLS // LANES
    cos = jnp.concatenate([cos_ref[...]] * reps, axis=1)
    sin = jnp.concatenate([sin_ref[...]] * reps, axis=1)
    k_ref[...] = (pk * cos + swapped * sin).astype(_bf16)
    gate_ref[...] = jax.nn.sigmoid(p[:, K_COLS:])

    pt = lax.dot_general(wt_ref[...], u, (((1,), (1,)), ((), ())),
                         preferred_element_type=_f32)
    tm = pt.shape[1]
    nq = QT_ROWS // HEAD_DIM
    q3 = pt[:QT_ROWS].reshape(nq, HEAD_DIM, tm)
    sw = jnp.concatenate([q3[:, half:], q3[:, :half]], axis=1)
    q3 = q3 * cost_ref[...][None] + sw * sint_ref[...][None]
    head = lax.broadcasted_iota(_i32, (nq, 1, 1), 0)
    q3 = q3 * jnp.where(head < SA_HEADS + SW_HEADS, LOG2E, 1.0)
    qt_ref[...] = q3.reshape(QT_ROWS, tm).astype(_bf16)
    vrow = lax.broadcasted_iota(_i32, (VT_ROWS, tm), 0)
    ones_row = jnp.zeros((VT_ROWS, tm), _f32)
    for g in range(VT_ROWS // V_GROUP):
        ones_row = jnp.where(vrow == g * V_GROUP + HEAD_DIM, 1.0, ones_row)
    vt_ref[...] = (pt[QT_ROWS:QT_ROWS + VT_ROWS] + ones_row).astype(_bf16)
    wit_ref[...] = pt[QT_ROWS + VT_ROWS:]


def _const_spec(shape):
    nd = len(shape)
    return pl.BlockSpec(shape, lambda *_: (0,) * nd, pipeline_mode=pl.Buffered(1))


def _front(xs, tabs, g1, gm, wg, wu, wd, wrow, wt, *, tm):
    B, S, D = xs.shape
    F = wg.shape[1]
    row_cos, row_sin, col_cos, col_sin = tabs
    gate_cols = wrow.shape[1] - K_COLS
    row_blk = lambda w: pl.BlockSpec((None, tm, w), lambda b, j: (b, j, 0))
    col_blk = lambda r: pl.BlockSpec((None, r, tm), lambda b, j: (b, 0, j))
    out_shape = (
        jax.ShapeDtypeStruct((B, S, D), _f32),
        jax.ShapeDtypeStruct((B, S, K_COLS), _bf16),
        jax.ShapeDtypeStruct((B, S, gate_cols), _f32),
        jax.ShapeDtypeStruct((B, QT_ROWS, S), _bf16),
        jax.ShapeDtypeStruct((B, VT_ROWS, S), _bf16),
        jax.ShapeDtypeStruct((B, WI_ROWS, S), _f32),
    )
    return pl.pallas_call(
        _front_kernel,
        out_shape=out_shape,
        grid=(B, S // tm),
        in_specs=[
            row_blk(D),
            pl.BlockSpec((tm, LANES), lambda b, j: (j, 0)),
            pl.BlockSpec((tm, LANES), lambda b, j: (j, 0)),
            pl.BlockSpec((HEAD_DIM, tm), lambda b, j: (0, j)),
            pl.BlockSpec((HEAD_DIM, tm), lambda b, j: (0, j)),
            _const_spec((1, D)), _const_spec((1, D)),
            _const_spec((D, F)), _const_spec((D, F)), _const_spec((F, D)),
            _const_spec(wrow.shape), _const_spec(wt.shape),
        ],
        out_specs=(row_blk(D), row_blk(K_COLS), row_blk(gate_cols),
                   col_blk(QT_ROWS), col_blk(VT_ROWS), col_blk(WI_ROWS)),
        compiler_params=pltpu.CompilerParams(
            dimension_semantics=("arbitrary", "arbitrary"),
            vmem_limit_bytes=VMEM_LIMIT_BYTES),
        name="front",
    )(xs, row_cos, row_sin, col_cos, col_sin, g1, gm, wg, wu, wd, wrow, wt)


def _sortable(bits):
    return bits ^ (lax.shift_right_arithmetic(bits, 31) & jnp.int32(0x7FFFFFFF))


def _key_to_float(key):
    return lax.bitcast_convert_type(_sortable(key), _f32)


def _float_to_key(x):
    return _sortable(lax.bitcast_convert_type(x, _i32))


def _sublane_all(x, op):
    x = op(x, pltpu.roll(x, 4, 0))
    x = op(x, pltpu.roll(x, 2, 0))
    return op(x, pltpu.roll(x, 1, 0))


def _tree(parts, op):
    while len(parts) > 1:
        parts = [op(parts[j], parts[j + 1]) if j + 1 < len(parts) else parts[j] for j in range(0, len(parts), 2)]
    return parts[0]


def _attn_kernel(sink_ref, qt_ref, wit_ref, k_ref, vt_ref, tri_ref, osp_ref, osw_ref,
                 st_scr, wq_scr, wsw_scr, rawa_scr, rawb_scr, acc_scr, *, topk):
    tq = BLOCK
    i = pl.program_id(1)
    nch = (i + 3) // 2
    nsl = (nch + 1) // 2
    n32 = KEY_CHUNK // SUBLANES
    s32 = SEARCH_CHUNK // SUBLANES
    krow = lax.broadcasted_iota(_i32, (KEY_CHUNK, tq), 0)
    qcol = BLOCK * (i + 1) + lax.broadcasted_iota(_i32, (KEY_CHUNK, tq), 1)

    def chunk_off(c):
        return pl.multiple_of(c * KEY_CHUNK, KEY_CHUNK)

    def slab_off(c):
        return pl.multiple_of(c * SEARCH_CHUNK, SEARCH_CHUNK)

    def pair_cols(pair):
        return slice(2 * pair * tq, 2 * (pair + 1) * tq)

    def store_pair(raw_scr, pair, s2):
        raw_scr[2 * pair] = s2[:, :tq]
        raw_scr[2 * pair + 1] = s2[:, tq:]


    zero_panel = jnp.zeros((HEAD_DIM, tq), _bf16)
    for h in range(IDX_HEADS):
        wq_scr[0:HEAD_DIM, h * tq:(h + 1) * tq] = zero_panel
        wq_scr[HEAD_DIM:, h * tq:(h + 1) * tq] = qt_ref[QI_ROW + h * HEAD_DIM:QI_ROW + (h + 1) * HEAD_DIM, :]
    wi = wit_ref[...]
    wrow = jnp.concatenate([jnp.broadcast_to(wi[h:h + 1], (SUBLANES, tq)) for h in range(IDX_HEADS)], axis=1)

    def index_dots(c, raw_scr):
        kc = k_ref[pl.ds(chunk_off(c), KEY_CHUNK), 0:2 * HEAD_DIM]
        for pair in range(IDX_HEADS // 2):
            store_pair(raw_scr, pair, jnp.dot(kc, wq_scr[:, pair_cols(pair)], preferred_element_type=_f32))

    def index_finish(c, raw_scr):
        off = chunk_off(c)
        terms = [jnp.maximum(raw_scr[h], 0.0).reshape(n32, SUBLANES, tq) * wrow[None, :, h * tq:(h + 1) * tq]
                 for h in range(IDX_HEADS)]
        sc = _tree(terms, jnp.add).reshape(KEY_CHUNK, tq)
        kidx = off + krow
        st_scr[pl.ds(off, KEY_CHUNK), :] = jnp.where((kidx >= N_PAD) & (kidx <= qcol), sc, NEG_INF)

    last_chunk = 2 * nsl - 1

    def index_body(j, carry):
        index_dots(2 * j + 1, rawb_scr)
        index_finish(2 * j, rawa_scr)
        index_dots(jnp.minimum(2 * j + 2, last_chunk), rawa_scr)
        index_finish(2 * j + 1, rawb_scr)
        return carry

    def store_heads(o_ref, outs):
        for j in range(0, len(outs), 2):
            o_ref[:, j * HEAD_DIM:(j + 2) * HEAD_DIM] = jnp.concatenate(outs[j:j + 2], axis=0).T.astype(o_ref.dtype)

    def sliding_window():
        koff = pl.multiple_of(i * BLOCK, BLOCK)
        kidx = koff + krow
        diff = qcol - kidx
        wmask = (kidx >= N_PAD) & (diff >= 0) & (diff < WINDOW)
        outs = []
        for g in range(SW_KV_HEADS):
            for j in range(SW_GROUP):
                hq = g * SW_GROUP + j
                for half in range(SW_KV_HEADS):
                    wsw_scr[half * HEAD_DIM:(half + 1) * HEAD_DIM, hq * tq:(hq + 1) * tq] = (
                        qt_ref[QS_ROW + hq * HEAD_DIM:QS_ROW + (hq + 1) * HEAD_DIM, :] if half == g else zero_panel)
            s = jnp.dot(k_ref[pl.ds(koff, 2 * BLOCK), 2 * HEAD_DIM:4 * HEAD_DIM],
                        wsw_scr[:, g * SW_GROUP * tq:(g + 1) * SW_GROUP * tq],
                        preferred_element_type=_f32)
            es, ms = [], []
            for j in range(SW_GROUP):
                sink = sink_ref[g * SW_GROUP + j] * LOG2E
                sh = jnp.where(wmask, s[:, j * tq:(j + 1) * tq], NEG_INF).reshape(n32, SUBLANES, tq)
                m = jnp.maximum(_sublane_all(_tree([sh[r] for r in range(n32)], jnp.maximum), jnp.maximum), sink)
                es.append(jnp.exp2(sh - m[None]).reshape(2 * BLOCK, tq).astype(_bf16))
                ms.append(jnp.exp2(sink - m[:1]))
            ev = jnp.dot(vt_ref[(1 + g) * V_GROUP:(2 + g) * V_GROUP, pl.ds(koff, 2 * BLOCK)],
                         jnp.concatenate(es, axis=1), preferred_element_type=_f32)
            for j in range(SW_GROUP):
                outs.append(ev[0:HEAD_DIM, j * tq:(j + 1) * tq]
                            / (ev[HEAD_DIM:HEAD_DIM + 1, j * tq:(j + 1) * tq] + ms[j]))
        store_heads(osw_ref, outs)

    index_dots(0, rawa_scr)
    sliding_window()
    lax.fori_loop(0, nsl, index_body, 0)

    def count32(*preds):
        def body(c, accs):
            st = st_scr[pl.ds(slab_off(c), SEARCH_CHUNK), :].reshape(s32 // 4, 4, SUBLANES, tq)
            out = []
            for pred, acc in zip(preds, accs):
                ones = jnp.where(pred(st), 1, 0).astype(_i32)
                out.append(acc + _tree([ones[j] for j in range(s32 // 4)], jnp.add))
            return tuple(out)
        accs = lax.fori_loop(0, nsl, body, tuple(jnp.zeros((4, SUBLANES, tq), _i32) for _ in preds))
        return [_sublane_all(_tree([acc[j] for j in range(4)], jnp.add), jnp.add) for acc in accs]

    def count_ge(key):
        t = _key_to_float(key)
        return count32(lambda st: st >= t)[0]

    def bound_body(c, acc):
        return jnp.maximum(acc, st_scr[pl.ds(chunk_off(c), KEY_CHUNK), :].reshape(n32, SUBLANES, tq))
    tops = lax.fori_loop(0, 2 * nsl, bound_body, jnp.full((n32, SUBLANES, tq), NEG_INF, _f32))
    ub = _sublane_all(_tree([tops[j] for j in range(n32)], jnp.maximum), jnp.maximum)
    lb = _sublane_all(_tree([tops[j] for j in range(n32)], jnp.minimum), jnp.minimum)

    def probe(lo, hi, mid):
        mid = jnp.minimum(jnp.maximum(mid, lo), hi - 1)
        cnt = count_ge(mid)
        ge = cnt >= topk
        return jnp.where(ge, mid, lo), jnp.where(cnt == topk, mid + 1, jnp.where(ge, hi, mid))

    def key_mid(lo, hi):
        return lax.shift_right_arithmetic(lo, 1) + lax.shift_right_arithmetic(hi, 1) + (lo & hi & 1)

    def value_mid(lo, hi):
        mid = _float_to_key(0.5 * (_key_to_float(lo) + _key_to_float(hi)))
        return jnp.where((mid > lo) & (mid < hi), mid, key_mid(lo, hi))

    c_ge0, c_gt0 = count32(lambda st: st >= 0.0, lambda st: st > 0.0)
    at_zero = (c_gt0 < topk) & (c_ge0 >= topk)
    lo = jnp.where(c_ge0 >= topk, 0, KEY_BELOW_ALL)
    hi = jnp.where(at_zero, 1, jnp.where(c_ge0 >= topk, KEY_ABOVE_ALL, 0))
    lo, hi = probe(lo, hi, _float_to_key(lb))
    lo, hi = probe(lo, hi, _float_to_key(ub) + 1)
    for _ in range(VALUE_BISECT_STEPS):
        lo, hi = probe(lo, hi, value_mid(lo, hi))

    def search_body(state):
        lo, hi, _ = state
        for _ in range(BISECT_STEPS_PER_CHECK):
            lo, hi = probe(lo, hi, key_mid(lo, hi))
        return lo, hi, jnp.max(jnp.where(hi - lo != 1, 1, 0).astype(_i32))

    pending = jnp.max(jnp.where(hi - lo != 1, 1, 0).astype(_i32))
    lo, hi, _ = lax.while_loop(lambda state: state[2] > 0, search_body, (lo, hi, pending))
    t8 = _key_to_float(lo)

    cnt_gt, cnt_ge = count32(lambda st: st > t8, lambda st: st >= t8)
    need = topk - cnt_gt
    tie_rows = (cnt_ge - cnt_gt > need) & (t8 >= SEL_FLOOR)
    has_ties = jnp.max(jnp.where(tie_rows, 1, 0).astype(_i32))

    @pl.when(has_ties > 0)
    def _():
        need_f = need.astype(_f32)

        def tie_dots(c, raw_scr):
            tie = jnp.where(st_scr[pl.ds(chunk_off(c), KEY_CHUNK), :].reshape(n32, SUBLANES, tq) == t8[None], 1.0, 0.0)
            raw_scr[0] = jnp.dot(tri_ref[...], tie.reshape(KEY_CHUNK, tq).astype(_bf16),
                                 preferred_element_type=_f32)

        def tie_finish(c, raw_scr, before):
            off = chunk_off(c)
            st = st_scr[pl.ds(off, KEY_CHUNK), :].reshape(n32, SUBLANES, tq)
            within = raw_scr[0].reshape(n32, SUBLANES, tq)
            drop = (st == t8[None]) & (within + before[None] > need_f[None])
            st_scr[pl.ds(off, KEY_CHUNK), :] = jnp.where(drop, NEG_INF, st).reshape(KEY_CHUNK, tq)
            return before + jnp.broadcast_to(within[n32 - 1, SUBLANES - 1:, :], (SUBLANES, tq))

        def tie_body(j, before):
            tie_dots(2 * j + 1, rawb_scr)
            before = tie_finish(2 * j, rawa_scr, before)
            tie_dots(jnp.minimum(2 * j + 2, last_chunk), rawa_scr)
            return tie_finish(2 * j + 1, rawb_scr, before)

        tie_dots(0, rawa_scr)
        lax.fori_loop(0, nsl, tie_body, jnp.zeros((SUBLANES, tq), _f32))

    tsel = jnp.maximum(t8, SEL_FLOOR)

    for h in range(SA_HEADS):
        wq_scr[0:HEAD_DIM, h * tq:(h + 1) * tq] = qt_ref[h * HEAD_DIM:(h + 1) * HEAD_DIM, :]
        wq_scr[HEAD_DIM:, h * tq:(h + 1) * tq] = zero_panel
    acc_scr[...] = jnp.zeros(acc_scr.shape, _f32)

    def score_dots(c, raw_scr):
        kc = k_ref[pl.ds(chunk_off(c), KEY_CHUNK), 0:2 * HEAD_DIM]
        for pair in range(SA_HEADS // 2):
            store_pair(raw_scr, pair, jnp.dot(kc, wq_scr[:, pair_cols(pair)], preferred_element_type=_f32))

    def chunk_update(c, raw_scr, m_old):
        off = chunk_off(c)
        sel = st_scr[pl.ds(off, KEY_CHUNK), :].reshape(n32, SUBLANES, tq) >= tsel[None]
        vc = vt_ref[0:V_GROUP, pl.ds(off, KEY_CHUNK)]
        m_new = []
        for pair in range(SA_HEADS // 2):
            ps, alphas = [], []
            for h in (2 * pair, 2 * pair + 1):
                sh = jnp.where(sel, raw_scr[h].reshape(n32, SUBLANES, tq), NEG_INF)
                mh = jnp.maximum(m_old[h], _sublane_all(_tree([sh[j] for j in range(n32)], jnp.maximum), jnp.maximum))
                ps.append(jnp.exp2(sh - mh[None]).reshape(KEY_CHUNK, tq).astype(_bf16))
                alphas.append(jnp.exp2(m_old[h] - mh)[:1])
                m_new.append(mh)
            acc_scr[:, pair_cols(pair)] = (jnp.concatenate(alphas, axis=1) * acc_scr[:, pair_cols(pair)]
                                           + jnp.dot(vc, jnp.concatenate(ps, axis=1), preferred_element_type=_f32))
        return m_new

    def attn_body(j, m):
        score_dots(2 * j + 1, rawb_scr)
        m = chunk_update(2 * j, rawa_scr, list(m))
        m = chunk_update(2 * j + 1, rawb_scr, m)
        score_dots(jnp.minimum(2 * j + 2, last_chunk), rawa_scr)
        return tuple(m)

    score_dots(0, rawa_scr)
    lax.fori_loop(0, nsl, attn_body, tuple(jnp.full((SUBLANES, tq), NEG_INF, _f32) for _ in range(SA_HEADS)))

    acc = acc_scr[...]
    store_heads(osp_ref, [acc[0:HEAD_DIM, h * tq:(h + 1) * tq] / acc[HEAD_DIM:HEAD_DIM + 1, h * tq:(h + 1) * tq]
                          for h in range(SA_HEADS)])


def _attention(sinks, qt, wit, k, vt, *, topk):
    B, _, S = qt.shape
    nkp = k.shape[1]
    tq = BLOCK
    kernel = functools.partial(_attn_kernel, topk=topk)
    out_blk = pl.BlockSpec((None, tq, SA_HEADS * HEAD_DIM), lambda b, i: (b, i, 0))
    return pl.pallas_call(
        kernel,
        out_shape=(jax.ShapeDtypeStruct((B, S, SA_HEADS * HEAD_DIM), _bf16),
                   jax.ShapeDtypeStruct((B, S, SW_HEADS * HEAD_DIM), _bf16)),
        grid=(B, S // tq),
        in_specs=[
            pl.BlockSpec(memory_space=pltpu.SMEM),
            pl.BlockSpec((None, QT_ROWS, tq), lambda b, i: (b, 0, i)),
            pl.BlockSpec((None, WI_ROWS, tq), lambda b, i: (b, 0, i)),
            pl.BlockSpec((None, nkp, K_COLS), lambda b, i: (b, 0, 0)),
            pl.BlockSpec((None, VT_ROWS, nkp), lambda b, i: (b, 0, 0)),
            _const_spec((KEY_CHUNK, KEY_CHUNK)),
        ],
        out_specs=(out_blk, out_blk),
        scratch_shapes=[
            pltpu.VMEM((nkp, tq), _f32),
            pltpu.VMEM((2 * HEAD_DIM, SA_HEADS * tq), _bf16),
            pltpu.VMEM((2 * HEAD_DIM, SW_HEADS * tq), _bf16),
            pltpu.VMEM((SA_HEADS, KEY_CHUNK, tq), _f32),
            pltpu.VMEM((SA_HEADS, KEY_CHUNK, tq), _f32),
            pltpu.VMEM((V_GROUP, SA_HEADS * tq), _f32),
        ],
        compiler_params=pltpu.CompilerParams(
            dimension_semantics=("arbitrary", "arbitrary"),
            vmem_limit_bytes=VMEM_LIMIT_BYTES),
        name="attn",
    )(sinks, qt, wit, k, vt, jnp.tril(jnp.ones((KEY_CHUNK, KEY_CHUNK), _bf16)))


def _back_kernel(h1_ref, osp_ref, osw_ref, gate_ref, wbs_ref, wbw_ref, wo_ref, g2_ref,
                 wg_ref, wu_ref, wd_ref, gf_ref, out_ref):
    d = h1_ref.shape[-1]
    a = jnp.dot(osp_ref[...], wbs_ref[...], preferred_element_type=_f32)
    b = jnp.dot(osw_ref[...], wbw_ref[...], preferred_element_type=_f32)
    gates = gate_ref[...]
    merged = (gates[:, :d] * a + gates[:, d:] * b).astype(_bf16)
    h2 = h1_ref[...] + jnp.dot(merged, wo_ref[...], preferred_element_type=_f32)
    h3 = _swiglu_half_step(h2, g2_ref[...], wg_ref, wu_ref, wd_ref)
    out_ref[...] = _rms(h3, gf_ref[...])


def _back(h1, osp, osw, gates, wbs, wbw, wo, g2, wg, wu, wd, gf, *, tm):
    B, S, D = h1.shape
    F = wg.shape[1]
    row_blk = lambda w: pl.BlockSpec((None, tm, w), lambda b, j: (b, j, 0))
    return pl.pallas_call(
        _back_kernel,
        out_shape=jax.ShapeDtypeStruct((B, S, D), _f32),
        grid=(B, S // tm),
        in_specs=[
            row_blk(D), row_blk(osp.shape[-1]), row_blk(osw.shape[-1]), row_blk(gates.shape[-1]),
            _const_spec(wbs.shape), _const_spec(wbw.shape), _const_spec(wo.shape), _const_spec((1, D)),
            _const_spec((D, F)), _const_spec((D, F)), _const_spec((F, D)), _const_spec((1, D)),
        ],
        out_specs=row_blk(D),
        compiler_params=pltpu.CompilerParams(
            dimension_semantics=("arbitrary", "arbitrary"),
            vmem_limit_bytes=VMEM_LIMIT_BYTES),
        name="back",
    )(h1, osp, osw, gates, wbs, wbw, wo, g2, wg, wu, wd, gf)


def _rope_tables(rpos):
    inv_freq = 1.0 / (ROPE_THETA ** (jnp.arange(0, HEAD_DIM, 2, dtype=_f32) / HEAD_DIM))
    ang = rpos.astype(_f32)[:, None] * inv_freq[None, :]
    cos, sin = jnp.cos(ang), jnp.sin(ang)
    cos64 = jnp.concatenate([cos, cos], axis=1)
    sin64 = jnp.concatenate([-sin, sin], axis=1)
    reps = LANES // HEAD_DIM
    return (jnp.tile(cos64, (1, reps)), jnp.tile(sin64, (1, reps)), cos64.T, sin64.T)


def _pack_w_in(w):
    sizes = (SA_HEADS * HEAD_DIM, HEAD_DIM, HEAD_DIM, IDX_HEADS * IDX_DIM, IDX_DIM, IDX_HEADS,
             SW_HEADS * HEAD_DIM, SW_KV_HEADS * HEAD_DIM, SW_KV_HEADS * HEAD_DIM)
    d = w.shape[0]
    gate_cols = w.shape[1] - sum(sizes)
    splits = [int(s) for s in np.cumsum(sizes + (gate_cols,))[:-1]]
    qa, ka, va, qi, ki, wi, qs, ksw, vsw, gates = jnp.split(w, splits, axis=1)
    scale = HEAD_DIM ** -0.5
    idx_scale = (IDX_HEADS ** -0.5) * (IDX_DIM ** -0.5)
    zeros = lambda n: jnp.zeros((d, n), w.dtype)
    wrow = jnp.concatenate([ka, ki, ksw, gates], axis=1)
    pad = zeros(V_GROUP - HEAD_DIM)
    wt = jnp.concatenate(
        [qa * scale, qs * scale, qi, va, pad]
        + [blk for g in range(SW_KV_HEADS) for blk in (vsw[:, g * HEAD_DIM:(g + 1) * HEAD_DIM], pad)]
        + [wi * idx_scale, zeros(WI_ROWS - wi.shape[1])], axis=1).T
    return wrow.astype(_bf16), wt.astype(_bf16)


def kernel(x, meta_tokens, norm_ffn1, w_ffn1_gate, w_ffn1_up, w_ffn1_down, norm_mix, w_in, sinks,
           w_branch_sparse, w_branch_swa, w_out, norm_ffn2, w_ffn2_gate, w_ffn2_up, w_ffn2_down, norm_final):
    B, S, D = x.shape
    assert norm_ffn1.shape[0] == 1, "single-layer stack only"
    assert S % BLOCK == 0
    topk = min(TOPK_MAX, S // 4)
    assert topk <= KEY_CHUNK
    tm = 256 if S % 256 == 0 else BLOCK
    bf = lambda a: a.astype(_bf16)

    wrow, wt = _pack_w_in(w_in[0])
    g1, gm, g2 = norm_ffn1[0][None], norm_mix[0][None], norm_ffn2[0][None]
    wg1, wu1, wd1 = bf(w_ffn1_gate[0]), bf(w_ffn1_up[0]), bf(w_ffn1_down[0])

    pos_real = N_META + jnp.arange(S, dtype=jnp.int32)
    pos_meta = jnp.maximum(jnp.arange(BLOCK, dtype=jnp.int32) - N_PAD, 0)
    h1, k, gates, qt, vt, wit = _front(x, _rope_tables(pos_real), g1, gm, wg1, wu1, wd1, wrow, wt, tm=tm)
    meta_blk = jnp.concatenate([jnp.zeros((N_PAD, D), x.dtype), meta_tokens.astype(x.dtype)], axis=0)[None]
    _, k_m, _, _, vt_m, _ = _front(meta_blk, _rope_tables(pos_meta), g1, gm, wg1, wu1, wd1, wrow, wt, tm=BLOCK)

    nkp = -(-(S + BLOCK) // SEARCH_CHUNK) * SEARCH_CHUNK
    k_all = jnp.concatenate([jnp.broadcast_to(k_m, (B, BLOCK, K_COLS)), k,
                             jnp.zeros((B, nkp - S - BLOCK, K_COLS), k.dtype)], axis=1)
    vt_all = jnp.concatenate([jnp.broadcast_to(vt_m, (B, VT_ROWS, BLOCK)), vt,
                              jnp.zeros((B, VT_ROWS, nkp - S - BLOCK), vt.dtype)], axis=2)

    o_sparse, o_swa = _attention(sinks[0].astype(_f32), qt, wit, k_all, vt_all, topk=topk)

    return _back(h1, o_sparse, o_swa, gates, bf(w_branch_sparse[0]), bf(w_branch_swa[0]), bf(w_out[0]), g2,
                 bf(w_ffn2_gate[0]), bf(w_ffn2_up[0]), bf(w_ffn2_down[0]), norm_final[None], tm=tm)
```

```python
import functools

import numpy as np
import jax
import jax.numpy as jnp
from jax import lax
from jax.experimental import pallas as pl
from jax.experimental.pallas import tpu as pltpu

N_META = 16
BLOCK = 128
HEAD_DIM = 64
ROPE_THETA = 10000.0
EPS = 1e-6
NEG_INF = -1e30
SA_HEADS = 8
TOPK_MAX = 256
IDX_HEADS = 4
IDX_DIM = 64
SW_HEADS = 8
SW_KV_HEADS = 2
WINDOW = 128
N_PAD = BLOCK - N_META
SW_GROUP = SW_HEADS // SW_KV_HEADS
LOG2E = 1.4426950408889634

LANES = 128
SUBLANES = 8
PACK16 = 16
KEY_CHUNK = 256
SEARCH_CHUNK = 2 * KEY_CHUNK
VMEM_LIMIT_BYTES = 56 * 1024 * 1024

QT_ROWS = (SA_HEADS + SW_HEADS + IDX_HEADS) * HEAD_DIM
QS_ROW = SA_HEADS * HEAD_DIM
QI_ROW = QS_ROW + SW_HEADS * HEAD_DIM
V_GROUP = HEAD_DIM + PACK16
VT_ROWS = (1 + SW_KV_HEADS) * V_GROUP
WI_ROWS = PACK16
PT_ROWS = QT_ROWS + VT_ROWS + WI_ROWS
K_COLS = 4 * HEAD_DIM
SEL_FLOOR = float(np.nextafter(np.float32(0.5 * NEG_INF), np.float32(0.0)))
KEY_BELOW_ALL = int(np.int32(np.uint32(0xFF800000) ^ np.uint32(0x7FFFFFFF)))
KEY_ABOVE_ALL = 0x7F800001
VALUE_BISECT_STEPS = 14
EXTRACT_PASSES = 3
BISECT_STEPS_PER_CHECK = 3

_f32 = jnp.float32
_bf16 = jnp.bfloat16
_i32 = jnp.int32


def _rms(x, g):
    return x * lax.rsqrt(jnp.mean(x * x, axis=-1, keepdims=True) + EPS) * g


def _swiglu_half_step(x, g_norm, wg_ref, wu_ref, wd_ref):
    a = _rms(x, g_norm).astype(_bf16)
    g = jnp.dot(a, wg_ref[...], preferred_element_type=_f32)
    u = jnp.dot(a, wu_ref[...], preferred_element_type=_f32)
    act = (g * jax.nn.sigmoid(g) * u).astype(_bf16)
    return x + 0.5 * jnp.dot(act, wd_ref[...], preferred_element_type=_f32)


def _front_kernel(x_ref, cos_ref, sin_ref, cost_ref, sint_ref, g1_ref, gm_ref,
                  wg_ref, wu_ref, wd_ref, wrow_ref, wt_ref,
                  h1_ref, k_ref, gate_ref, qt_ref, vt_ref, wit_ref):
    h1 = _swiglu_half_step(x_ref[...], g1_ref[...], wg_ref, wu_ref, wd_ref)
    h1_ref[...] = h1
    u = _rms(h1, gm_ref[...]).astype(_bf16)
    half = HEAD_DIM // 2

    p = jnp.dot(u, wrow_ref[...], preferred_element_type=_f32)
    pk = p[:, :K_COLS]
    lane = lax.broadcasted_iota(_i32, pk.shape, 1)
    swapped = jnp.where((lane & (HEAD_DIM - 1)) < half,
                        pltpu.roll(pk, K_COLS - half, 1), pltpu.roll(pk, half, 1))
    reps = K_COLS // LANES
    cos = jnp.concatenate([cos_ref[...]] * reps, axis=1)
    sin = jnp.concatenate([sin_ref[...]] * reps, axis=1)
    k_ref[...] = (pk * cos + swapped * sin).astype(_bf16)
    gate_ref[...] = jax.nn.sigmoid(p[:, K_COLS:])

    pt = lax.dot_general(wt_ref[...], u, (((1,), (1,)), ((), ())),
                         preferred_element_type=_f32)
    tm = pt.shape[1]
    nq = QT_ROWS // HEAD_DIM
    q3 = pt[:QT_ROWS].reshape(nq, HEAD_DIM, tm)
    sw = jnp.concatenate([q3[:, half:], q3[:, :half]], axis=1)
    q3 = q3 * cost_ref[...][None] + sw * sint_ref[...][None]
    head = lax.broadcasted_iota(_i32, (nq, 1, 1), 0)
    q3 = q3 * jnp.where(head < SA_HEADS + SW_HEADS, LOG2E, 1.0)
    qt_ref[...] = q3.reshape(QT_ROWS, tm).astype(_bf16)
    vrow = lax.broadcasted_iota(_i32, (VT_ROWS, tm), 0)
    ones_row = jnp.zeros((VT_ROWS, tm), _f32)
    for g in range(VT_ROWS // V_GROUP):
        ones_row = jnp.where(vrow == g * V_GROUP + HEAD_DIM, 1.0, ones_row)
    vt_ref[...] = (pt[QT_ROWS:QT_ROWS + VT_ROWS] + ones_row).astype(_bf16)
    wit_ref[...] = pt[QT_ROWS + VT_ROWS:]


def _const_spec(shape):
    nd = len(shape)
    return pl.BlockSpec(shape, lambda *_: (0,) * nd, pipeline_mode=pl.Buffered(1))


def _front(xs, tabs, g1, gm, wg, wu, wd, wrow, wt, *, tm):
    B, S, D = xs.shape
    F = wg.shape[1]
    row_cos, row_sin, col_cos, col_sin = tabs
    gate_cols = wrow.shape[1] - K_COLS
    row_blk = lambda w: pl.BlockSpec((None, tm, w), lambda b, j: (b, j, 0))
    col_blk = lambda r: pl.BlockSpec((None, r, tm), lambda b, j: (b, 0, j))
    out_shape = (
        jax.ShapeDtypeStruct((B, S, D), _f32),
        jax.ShapeDtypeStruct((B, S, K_COLS), _bf16),
        jax.ShapeDtypeStruct((B, S, gate_cols), _f32),
        jax.ShapeDtypeStruct((B, QT_ROWS, S), _bf16),
        jax.ShapeDtypeStruct((B, VT_ROWS, S), _bf16),
        jax.ShapeDtypeStruct((B, WI_ROWS, S), _f32),
    )
    return pl.pallas_call(
        _front_kernel,
        out_shape=out_shape,
        grid=(B, S // tm),
        in_specs=[
            row_blk(D),
            pl.BlockSpec((tm, LANES), lambda b, j: (j, 0)),
            pl.BlockSpec((tm, LANES), lambda b, j: (j, 0)),
            pl.BlockSpec((HEAD_DIM, tm), lambda b, j: (0, j)),
            pl.BlockSpec((HEAD_DIM, tm), lambda b, j: (0, j)),
            _const_spec((1, D)), _const_spec((1, D)),
            _const_spec((D, F)), _const_spec((D, F)), _const_spec((F, D)),
            _const_spec(wrow.shape), _const_spec(wt.shape),
        ],
        out_specs=(row_blk(D), row_blk(K_COLS), row_blk(gate_cols),
                   col_blk(QT_ROWS), col_blk(VT_ROWS), col_blk(WI_ROWS)),
        compiler_params=pltpu.CompilerParams(
            dimension_semantics=("arbitrary", "arbitrary"),
            vmem_limit_bytes=VMEM_LIMIT_BYTES),
        name="front",
    )(xs, row_cos, row_sin, col_cos, col_sin, g1, gm, wg, wu, wd, wrow, wt)


def _sortable(bits):
    return bits ^ (lax.shift_right_arithmetic(bits, 31) & jnp.int32(0x7FFFFFFF))


def _np_key(x):
    bits = np.array(x, np.float32).view(np.int32)
    return bits ^ ((bits >> 31) & np.int32(0x7FFFFFFF))


def _key_to_float(key):
    return lax.bitcast_convert_type(_sortable(key), _f32)


def _float_to_key(x):
    return _sortable(lax.bitcast_convert_type(x, _i32))


def _sublane_all(x, op):
    x = op(x, pltpu.roll(x, 4, 0))
    x = op(x, pltpu.roll(x, 2, 0))
    return op(x, pltpu.roll(x, 1, 0))


def _tree(parts, op):
    while len(parts) > 1:
        parts = [op(parts[j], parts[j + 1]) if j + 1 < len(parts) else parts[j] for j in range(0, len(parts), 2)]
    return parts[0]


def _attn_kernel(sink_ref, qt_ref, wit_ref, k_ref, vt_ref, tri_ref, osp_ref, osw_ref,
                 st_scr, wq_scr, wsw_scr, rawa_scr, rawb_scr, tiea_scr, tieb_scr, acc_scr, *, topk):
    tq = BLOCK
    i = pl.program_id(1)
    nch = (i + 3) // 2
    nsl = (nch + 1) // 2
    n32 = KEY_CHUNK // SUBLANES
    s32 = SEARCH_CHUNK // SUBLANES
    krow = lax.broadcasted_iota(_i32, (KEY_CHUNK, tq), 0)
    qcol = BLOCK * (i + 1) + lax.broadcasted_iota(_i32, (KEY_CHUNK, tq), 1)

    def chunk_off(c):
        return pl.multiple_of(c * KEY_CHUNK, KEY_CHUNK)

    def slab_off(c):
        return pl.multiple_of(c * SEARCH_CHUNK, SEARCH_CHUNK)

    def pair_cols(pair):
        return slice(2 * pair * tq, 2 * (pair + 1) * tq)

    def store_pair(raw_scr, pair, s2):
        raw_scr[2 * pair] = s2[:, :tq]
        raw_scr[2 * pair + 1] = s2[:, tq:]


    zero_panel = jnp.zeros((HEAD_DIM, tq), _bf16)
    for h in range(IDX_HEADS):
        wq_scr[0:HEAD_DIM, h * tq:(h + 1) * tq] = zero_panel
        wq_scr[HEAD_DIM:, h * tq:(h + 1) * tq] = qt_ref[QI_ROW + h * HEAD_DIM:QI_ROW + (h + 1) * HEAD_DIM, :]
    wi = wit_ref[...]
    wrow = jnp.concatenate([jnp.broadcast_to(wi[h:h + 1], (SUBLANES, tq)) for h in range(IDX_HEADS)], axis=1)

    def index_dots(c, raw_scr):
        kc = k_ref[pl.ds(chunk_off(c), KEY_CHUNK), 0:2 * HEAD_DIM]
        for pair in range(IDX_HEADS // 2):
            store_pair(raw_scr, pair, jnp.dot(kc, wq_scr[:, pair_cols(pair)], preferred_element_type=_f32))

    def index_finish(c, raw_scr, top):
        off = chunk_off(c)
        terms = [jnp.maximum(raw_scr[h], 0.0).reshape(n32, SUBLANES, tq) * wrow[None, :, h * tq:(h + 1) * tq]
                 for h in range(IDX_HEADS)]
        kidx = off + krow
        sc = jnp.where((kidx >= N_PAD) & (kidx <= qcol), _tree(terms, jnp.add).reshape(KEY_CHUNK, tq), NEG_INF)
        st_scr[pl.ds(off, KEY_CHUNK), :] = sc
        sc = sc.reshape(n32, SUBLANES, tq)
        return jnp.maximum(top, _tree([sc[j] for j in range(n32)], jnp.maximum))

    last_chunk = 2 * nsl - 1

    def index_body(j, top):
        index_dots(2 * j + 1, rawb_scr)
        top = index_finish(2 * j, rawa_scr, top)
        index_dots(jnp.minimum(2 * j + 2, last_chunk), rawa_scr)
        return index_finish(2 * j + 1, rawb_scr, top)

    def store_heads(o_ref, outs):
        for j in range(0, len(outs), 2):
            o_ref[:, j * HEAD_DIM:(j + 2) * HEAD_DIM] = jnp.concatenate(outs[j:j + 2], axis=0).T.astype(o_ref.dtype)

    def sliding_window():
        koff = pl.multiple_of(i * BLOCK, BLOCK)
        kidx = koff + krow
        diff = qcol - kidx
        wmask = (kidx >= N_PAD) & (diff >= 0) & (diff < WINDOW)
        outs = []
        for g in range(SW_KV_HEADS):
            for j in range(SW_GROUP):
                hq = g * SW_GROUP + j
                for half in range(SW_KV_HEADS):
                    wsw_scr[half * HEAD_DIM:(half + 1) * HEAD_DIM, hq * tq:(hq + 1) * tq] = (
                        qt_ref[QS_ROW + hq * HEAD_DIM:QS_ROW + (hq + 1) * HEAD_DIM, :] if half == g else zero_panel)
            s = jnp.dot(k_ref[pl.ds(koff, 2 * BLOCK), 2 * HEAD_DIM:4 * HEAD_DIM],
                        wsw_scr[:, g * SW_GROUP * tq:(g + 1) * SW_GROUP * tq],
                        preferred_element_type=_f32)
            es, ms = [], []
            for j in range(SW_GROUP):
                sink = sink_ref[g * SW_GROUP + j] * LOG2E
                sh = jnp.where(wmask, s[:, j * tq:(j + 1) * tq], NEG_INF).reshape(n32, SUBLANES, tq)
                m = jnp.maximum(_sublane_all(_tree([sh[r] for r in range(n32)], jnp.maximum), jnp.maximum), sink)
                es.append(jnp.exp2(sh - m[None]).reshape(2 * BLOCK, tq).astype(_bf16))
                ms.append(jnp.exp2(sink - m[:1]))
            ev = jnp.dot(vt_ref[(1 + g) * V_GROUP:(2 + g) * V_GROUP, pl.ds(koff, 2 * BLOCK)],
                         jnp.concatenate(es, axis=1), preferred_element_type=_f32)
            for j in range(SW_GROUP):
                outs.append(ev[0:HEAD_DIM, j * tq:(j + 1) * tq]
                            / (ev[HEAD_DIM:HEAD_DIM + 1, j * tq:(j + 1) * tq] + ms[j]))
        store_heads(osw_ref, outs)

    index_dots(0, rawa_scr)
    sliding_window()
    top = lax.fori_loop(0, nsl, index_body, jnp.full((SUBLANES, tq), NEG_INF, _f32))

    def count32(*preds):
        def body(c, accs):
            st = st_scr[pl.ds(slab_off(c), SEARCH_CHUNK), :].reshape(s32 // 4, 4, SUBLANES, tq)
            out = []
            for pred, acc in zip(preds, accs):
                ones = jnp.where(pred(st), 1, 0).astype(_i32)
                out.append(acc + _tree([ones[j] for j in range(s32 // 4)], jnp.add))
            return tuple(out)
        accs = lax.fori_loop(0, nsl, body, tuple(jnp.zeros((4, SUBLANES, tq), _i32) for _ in preds))
        return [_sublane_all(_tree([acc[j] for j in range(4)], jnp.add), jnp.add) for acc in accs]

    def count_ge(key):
        t = _key_to_float(key)
        return count32(lambda st: st >= t)[0]

    ub = _sublane_all(top, jnp.maximum)

    def probe(lo, hi, chi, mid):
        mid = jnp.minimum(jnp.maximum(mid, lo), hi - 1)
        cnt = count_ge(mid)
        ge = cnt >= topk
        return (jnp.where(ge, mid, lo), jnp.where(cnt == topk, mid + 1, jnp.where(ge, hi, mid)),
                jnp.where(ge, chi, cnt))

    def key_mid(lo, hi):
        return lax.shift_right_arithmetic(lo, 1) + lax.shift_right_arithmetic(hi, 1) + (lo & hi & 1)

    def value_mid(lo, hi):
        mid = _float_to_key(0.5 * (_key_to_float(lo) + _key_to_float(hi)))
        return jnp.where((mid > lo) & (mid < hi), mid, key_mid(lo, hi))

    def max_below(t):
        def body(c, acc):
            st = st_scr[pl.ds(slab_off(c), SEARCH_CHUNK), :].reshape(s32 // 4, 4, SUBLANES, tq)
            v = jnp.where(st < t, st, -jnp.inf)
            return jnp.maximum(acc, _tree([v[j] for j in range(s32 // 4)], jnp.maximum))
        acc = lax.fori_loop(0, nsl, body, jnp.full((4, SUBLANES, tq), -jnp.inf, _f32))
        return _sublane_all(_tree([acc[j] for j in range(4)], jnp.maximum), jnp.maximum)

    def unsettled(lo, hi):
        return jnp.max(jnp.where(hi - lo != 1, 1, 0).astype(_i32))

    c_ge0, c_gt0, c_adm = count32(lambda st: st >= 0.0, lambda st: st > 0.0, lambda st: st >= SEL_FLOOR)
    nonneg = c_ge0 >= topk
    at_zero = (c_gt0 < topk) & nonneg
    take_all = c_adm <= topk
    key_neg = int(_np_key(NEG_INF))
    lo = jnp.where(take_all, key_neg, jnp.where(nonneg, 0, KEY_BELOW_ALL))
    hi = jnp.where(take_all, key_neg + 1, jnp.where(at_zero, 1, jnp.where(nonneg, KEY_ABOVE_ALL, 0)))
    chi = jnp.where(nonneg, 0, c_ge0)
    lo, hi, chi = probe(lo, hi, chi, _float_to_key(ub) + 1)
    for _ in range(VALUE_BISECT_STEPS):
        lo, hi, chi = probe(lo, hi, chi, value_mid(lo, hi))

    settled = hi - lo == 1
    left = jnp.where(settled, 0, topk - chi)
    t8 = jnp.where(settled, _key_to_float(lo), _key_to_float(hi))
    for _ in range(EXTRACT_PASSES):
        below = max_below(t8)
        t8 = jnp.where(left > 0, below, t8)
        left = jnp.maximum(left - 1, 0)
    cnt_gt, cnt_ge = count32(lambda st: st > t8, lambda st: st >= t8)
    good = take_all | (settled | (left == 0)) & (cnt_ge >= topk) & ((cnt_gt < topk) | (cnt_ge == topk))

    hi = jnp.where(good, lo + 1, hi)
    fallback = unsettled(lo, hi)

    def search_body(state):
        lo, hi, chi, _ = state
        for _ in range(BISECT_STEPS_PER_CHECK):
            lo, hi, chi = probe(lo, hi, chi, key_mid(lo, hi))
        return lo, hi, chi, unsettled(lo, hi)

    lo, hi, chi, _ = lax.while_loop(lambda state: state[3] > 0, search_body, (lo, hi, chi, fallback))
    t8 = jnp.where(good, t8, _key_to_float(lo))

    cnt_gt, cnt_ge = lax.cond(fallback > 0,
                              lambda: tuple(count32(lambda st: st > t8, lambda st: st >= t8)),
                              lambda: (cnt_gt, cnt_ge))
    need = topk - cnt_gt
    tie_rows = (cnt_ge - cnt_gt > need) & (t8 >= SEL_FLOOR)
    has_ties = jnp.max(jnp.where(tie_rows, 1, 0).astype(_i32))

    for h in range(SA_HEADS):
        wq_scr[0:HEAD_DIM, h * tq:(h + 1) * tq] = qt_ref[h * HEAD_DIM:(h + 1) * HEAD_DIM, :]
        wq_scr[HEAD_DIM:, h * tq:(h + 1) * tq] = zero_panel
    acc_scr[...] = jnp.zeros(acc_scr.shape, _f32)

    def score_dots(c, raw_scr):
        kc = k_ref[pl.ds(chunk_off(c), KEY_CHUNK), 0:2 * HEAD_DIM]
        for pair in range(SA_HEADS // 2):
            store_pair(raw_scr, pair, jnp.dot(kc, wq_scr[:, pair_cols(pair)], preferred_element_type=_f32))

    score_dots(0, rawa_scr)

    @pl.when(has_ties > 0)
    def _():
        need_f = need.astype(_f32)

        def tie_dots(c, raw_scr):
            tie = jnp.where(st_scr[pl.ds(chunk_off(c), KEY_CHUNK), :].reshape(n32, SUBLANES, tq) == t8[None], 1.0, 0.0)
            raw_scr[...] = jnp.dot(tri_ref[...], tie.reshape(KEY_CHUNK, tq).astype(_bf16),
                                   preferred_element_type=_f32)

        def tie_finish(c, raw_scr, before):
            off = chunk_off(c)
            st = st_scr[pl.ds(off, KEY_CHUNK), :].reshape(n32, SUBLANES, tq)
            within = raw_scr[...].reshape(n32, SUBLANES, tq)
            drop = (st == t8[None]) & (within + before[None] > need_f[None])
            st_scr[pl.ds(off, KEY_CHUNK), :] = jnp.where(drop, NEG_INF, st).reshape(KEY_CHUNK, tq)
            return before + jnp.broadcast_to(within[n32 - 1, SUBLANES - 1:, :], (SUBLANES, tq))

        def tie_body(j, before):
            tie_dots(2 * j + 1, tieb_scr)
            before = tie_finish(2 * j, tiea_scr, before)
            tie_dots(jnp.minimum(2 * j + 2, last_chunk), tiea_scr)
            return tie_finish(2 * j + 1, tieb_scr, before)

        tie_dots(0, tiea_scr)
        lax.fori_loop(0, nsl, tie_body, jnp.zeros((SUBLANES, tq), _f32))

    tsel = jnp.maximum(t8, SEL_FLOOR)

    def chunk_update(c, raw_scr, m_old):
        off = chunk_off(c)
        sel = st_scr[pl.ds(off, KEY_CHUNK), :].reshape(n32, SUBLANES, tq) >= tsel[None]
        vc = vt_ref[0:V_GROUP, pl.ds(off, KEY_CHUNK)]
        m_new = []
        for pair in range(SA_HEADS // 2):
            ps, alphas = [], []
            for h in (2 * pair, 2 * pair + 1):
                sh = jnp.where(sel, raw_scr[h].reshape(n32, SUBLANES, tq), NEG_INF)
                mh = jnp.maximum(m_old[h], _sublane_all(_tree([sh[j] for j in range(n32)], jnp.maximum), jnp.maximum))
                ps.append(jnp.exp2(sh - mh[None]).reshape(KEY_CHUNK, tq).astype(_bf16))
                alphas.append(jnp.exp2(m_old[h] - mh)[:1])
                m_new.append(mh)
            acc_scr[:, pair_cols(pair)] = (jnp.concatenate(alphas, axis=1) * acc_scr[:, pair_cols(pair)]
                                           + jnp.dot(vc, jnp.concatenate(ps, axis=1), preferred_element_type=_f32))
        return m_new

    def attn_body(j, m):
        score_dots(2 * j + 1, rawb_scr)
        m = chunk_update(2 * j, rawa_scr, list(m))
        m = chunk_update(2 * j + 1, rawb_scr, m)
        score_dots(jnp.minimum(2 * j + 2, last_chunk), rawa_scr)
        return tuple(m)

    lax.fori_loop(0, nsl, attn_body, tuple(jnp.full((SUBLANES, tq), NEG_INF, _f32) for _ in range(SA_HEADS)))

    acc = acc_scr[...]
    store_heads(osp_ref, [acc[0:HEAD_DIM, h * tq:(h + 1) * tq] / acc[HEAD_DIM:HEAD_DIM + 1, h * tq:(h + 1) * tq]
                          for h in range(SA_HEADS)])


def _attention(sinks, qt, wit, k, vt, *, topk):
    B, _, S = qt.shape
    nkp = k.shape[1]
    tq = BLOCK
    kernel = functools.partial(_attn_kernel, topk=topk)
    out_blk = pl.BlockSpec((None, tq, SA_HEADS * HEAD_DIM), lambda b, i: (b, i, 0))
    return pl.pallas_call(
        kernel,
        out_shape=(jax.ShapeDtypeStruct((B, S, SA_HEADS * HEAD_DIM), _bf16),
                   jax.ShapeDtypeStruct((B, S, SW_HEADS * HEAD_DIM), _bf16)),
        grid=(B, S // tq),
        in_specs=[
            pl.BlockSpec(memory_space=pltpu.SMEM),
            pl.BlockSpec((None, QT_ROWS, tq), lambda b, i: (b, 0, i)),
            pl.BlockSpec((None, WI_ROWS, tq), lambda b, i: (b, 0, i)),
            pl.BlockSpec((None, nkp, K_COLS), lambda b, i: (b, 0, 0)),
            pl.BlockSpec((None, VT_ROWS, nkp), lambda b, i: (b, 0, 0)),
            _const_spec((KEY_CHUNK, KEY_CHUNK)),
        ],
        out_specs=(out_blk, out_blk),
        scratch_shapes=[
            pltpu.VMEM((nkp, tq), _f32),
            pltpu.VMEM((2 * HEAD_DIM, SA_HEADS * tq), _bf16),
            pltpu.VMEM((2 * HEAD_DIM, SW_HEADS * tq), _bf16),
            pltpu.VMEM((SA_HEADS, KEY_CHUNK, tq), _f32),
            pltpu.VMEM((SA_HEADS, KEY_CHUNK, tq), _f32),
            pltpu.VMEM((KEY_CHUNK, tq), _f32),
            pltpu.VMEM((KEY_CHUNK, tq), _f32),
            pltpu.VMEM((V_GROUP, SA_HEADS * tq), _f32),
        ],
        compiler_params=pltpu.CompilerParams(
            dimension_semantics=("arbitrary", "arbitrary"),
            vmem_limit_bytes=VMEM_LIMIT_BYTES),
        name="attn",
    )(sinks, qt, wit, k, vt, jnp.tril(jnp.ones((KEY_CHUNK, KEY_CHUNK), _bf16)))


def _back_kernel(h1_ref, osp_ref, osw_ref, gate_ref, wbs_ref, wbw_ref, wo_ref, g2_ref,
                 wg_ref, wu_ref, wd_ref, gf_ref, out_ref):
    d = h1_ref.shape[-1]
    a = jnp.dot(osp_ref[...], wbs_ref[...], preferred_element_type=_f32)
    b = jnp.dot(osw_ref[...], wbw_ref[...], preferred_element_type=_f32)
    gates = gate_ref[...]
    merged = (gates[:, :d] * a + gates[:, d:] * b).astype(_bf16)
    h2 = h1_ref[...] + jnp.dot(merged, wo_ref[...], preferred_element_type=_f32)
    h3 = _swiglu_half_step(h2, g2_ref[...], wg_ref, wu_ref, wd_ref)
    out_ref[...] = _rms(h3, gf_ref[...])


def _back(h1, osp, osw, gates, wbs, wbw, wo, g2, wg, wu, wd, gf, *, tm):
    B, S, D = h1.shape
    F = wg.shape[1]
    row_blk = lambda w: pl.BlockSpec((None, tm, w), lambda b, j: (b, j, 0))
    return pl.pallas_call(
        _back_kernel,
        out_shape=jax.ShapeDtypeStruct((B, S, D), _f32),
        grid=(B, S // tm),
        in_specs=[
            row_blk(D), row_blk(osp.shape[-1]), row_blk(osw.shape[-1]), row_blk(gates.shape[-1]),
            _const_spec(wbs.shape), _const_spec(wbw.shape), _const_spec(wo.shape), _const_spec((1, D)),
            _const_spec((D, F)), _const_spec((D, F)), _const_spec((F, D)), _const_spec((1, D)),
        ],
        out_specs=row_blk(D),
        compiler_params=pltpu.CompilerParams(
            dimension_semantics=("arbitrary", "arbitrary"),
            vmem_limit_bytes=VMEM_LIMIT_BYTES),
        name="back",
    )(h1, osp, osw, gates, wbs, wbw, wo, g2, wg, wu, wd, gf)


def _rope_tables(rpos):
    inv_freq = 1.0 / (ROPE_THETA ** (jnp.arange(0, HEAD_DIM, 2, dtype=_f32) / HEAD_DIM))
    ang = rpos.astype(_f32)[:, None] * inv_freq[None, :]
    cos, sin = jnp.cos(ang), jnp.sin(ang)
    cos64 = jnp.concatenate([cos, cos], axis=1)
    sin64 = jnp.concatenate([-sin, sin], axis=1)
    reps = LANES // HEAD_DIM
    return (jnp.tile(cos64, (1, reps)), jnp.tile(sin64, (1, reps)), cos64.T, sin64.T)


def _pack_w_in(w):
    sizes = (SA_HEADS * HEAD_DIM, HEAD_DIM, HEAD_DIM, IDX_HEADS * IDX_DIM, IDX_DIM, IDX_HEADS,
             SW_HEADS * HEAD_DIM, SW_KV_HEADS * HEAD_DIM, SW_KV_HEADS * HEAD_DIM)
    d = w.shape[0]
    gate_cols = w.shape[1] - sum(sizes)
    splits = [int(s) for s in np.cumsum(sizes + (gate_cols,))[:-1]]
    qa, ka, va, qi, ki, wi, qs, ksw, vsw, gates = jnp.split(w, splits, axis=1)
    scale = HEAD_DIM ** -0.5
    idx_scale = (IDX_HEADS ** -0.5) * (IDX_DIM ** -0.5)
    zeros = lambda n: jnp.zeros((d, n), w.dtype)
    wrow = jnp.concatenate([ka, ki, ksw, gates], axis=1)
    pad = zeros(V_GROUP - HEAD_DIM)
    wt = jnp.concatenate(
        [qa * scale, qs * scale, qi, va, pad]
        + [blk for g in range(SW_KV_HEADS) for blk in (vsw[:, g * HEAD_DIM:(g + 1) * HEAD_DIM], pad)]
        + [wi * idx_scale, zeros(WI_ROWS - wi.shape[1])], axis=1).T
    return wrow.astype(_bf16), wt.astype(_bf16)


def kernel(x, meta_tokens, norm_ffn1, w_ffn1_gate, w_ffn1_up, w_ffn1_down, norm_mix, w_in, sinks,
           w_branch_sparse, w_branch_swa, w_out, norm_ffn2, w_ffn2_gate, w_ffn2_up, w_ffn2_down, norm_final):
    B, S, D = x.shape
    assert norm_ffn1.shape[0] == 1, "single-layer stack only"
    assert S % BLOCK == 0
    topk = min(TOPK_MAX, S // 4)
    tm = 256 if S % 256 == 0 else BLOCK
    bf = lambda a: a.astype(_bf16)

    wrow, wt = _pack_w_in(w_in[0])
    g1, gm, g2 = norm_ffn1[0][None], norm_mix[0][None], norm_ffn2[0][None]
    wg1, wu1, wd1 = bf(w_ffn1_gate[0]), bf(w_ffn1_up[0]), bf(w_ffn1_down[0])

    pos_real = N_META + jnp.arange(S, dtype=jnp.int32)
    pos_meta = jnp.maximum(jnp.arange(BLOCK, dtype=jnp.int32) - N_PAD, 0)
    h1, k, gates, qt, vt, wit = _front(x, _rope_tables(pos_real), g1, gm, wg1, wu1, wd1, wrow, wt, tm=tm)
    meta_blk = jnp.concatenate([jnp.zeros((N_PAD, D), x.dtype), meta_tokens.astype(x.dtype)], axis=0)[None]
    _, k_m, _, _, vt_m, _ = _front(meta_blk, _rope_tables(pos_meta), g1, gm, wg1, wu1, wd1, wrow, wt, tm=BLOCK)

    nkp = -(-(S + BLOCK) // SEARCH_CHUNK) * SEARCH_CHUNK
    k_all = jnp.concatenate([jnp.broadcast_to(k_m, (B, BLOCK, K_COLS)), k,
                             jnp.zeros((B, nkp - S - BLOCK, K_COLS), k.dtype)], axis=1)
    vt_all = jnp.concatenate([jnp.broadcast_to(vt_m, (B, VT_ROWS, BLOCK)), vt,
                              jnp.zeros((B, VT_ROWS, nkp - S - BLOCK), vt.dtype)], axis=2)

    o_sparse, o_swa = _attention(sinks[0].astype(_f32), qt, wit, k_all, vt_all, topk=topk)

    return _back(h1, o_sparse, o_swa, gates, bf(w_branch_sparse[0]), bf(w_branch_swa[0]), bf(w_out[0]), g2,
                 bf(w_ffn2_gate[0]), bf(w_ffn2_up[0]), bf(w_ffn2_down[0]), norm_final[None], tm=tm)
```

```python
import functools

import numpy as np
import jax
import jax.numpy as jnp
from jax import lax
from jax.experimental import pallas as pl
from jax.experimental.pallas import tpu as pltpu

N_META = 16
BLOCK = 128
HEAD_DIM = 64
ROPE_THETA = 10000.0
EPS = 1e-6
NEG_INF = -1e30
SA_HEADS = 8
TOPK_MAX = 256
IDX_HEADS = 4
IDX_DIM = 64
SW_HEADS = 8
SW_KV_HEADS = 2
WINDOW = 128
N_PAD = BLOCK - N_META
SW_GROUP = SW_HEADS // SW_KV_HEADS
LOG2E = 1.4426950408889634

LANES = 128
SUBLANES = 8
PACK16 = 16
KEY_CHUNK = 256
SEARCH_CHUNK = 2 * KEY_CHUNK
VMEM_LIMIT_BYTES = 56 * 1024 * 1024

QT_ROWS = (SA_HEADS + SW_HEADS + IDX_HEADS) * HEAD_DIM
QS_ROW = SA_HEADS * HEAD_DIM
QI_ROW = QS_ROW + SW_HEADS * HEAD_DIM
V_GROUP = HEAD_DIM + PACK16
VT_ROWS = (1 + SW_KV_HEADS) * V_GROUP
WI_ROWS = PACK16
PT_ROWS = QT_ROWS + VT_ROWS + WI_ROWS
K_COLS = 4 * HEAD_DIM
SEL_FLOOR = float(np.nextafter(np.float32(0.5 * NEG_INF), np.float32(0.0)))
KEY_BELOW_ALL = int(np.int32(np.uint32(0xFF800000) ^ np.uint32(0x7FFFFFFF)))
KEY_ABOVE_ALL = 0x7F800001
VALUE_BISECT_STEPS = 14
EXTRACT_PASSES = 3
BISECT_STEPS_PER_CHECK = 3

_f32 = jnp.float32
_bf16 = jnp.bfloat16
_i32 = jnp.int32


def _rms(x, g):
    return x * lax.rsqrt(jnp.mean(x * x, axis=-1, keepdims=True) + EPS) * g


def _swiglu_half_step(x, g_norm, wg_ref, wu_ref, wd_ref):
    a = _rms(x, g_norm).astype(_bf16)
    g = jnp.dot(a, wg_ref[...], preferred_element_type=_f32)
    u = jnp.dot(a, wu_ref[...], preferred_element_type=_f32)
    act = (g * jax.nn.sigmoid(g) * u).astype(_bf16)
    return x + 0.5 * jnp.dot(act, wd_ref[...], preferred_element_type=_f32)


def _front_kernel(x_ref, cos_ref, sin_ref, cost_ref, sint_ref, g1_ref, gm_ref,
                  wg_ref, wu_ref, wd_ref, wrow_ref, wt_ref,
                  h1_ref, k_ref, gate_ref, qt_ref, vt_ref, wit_ref):
    h1 = _swiglu_half_step(x_ref[...], g1_ref[...], wg_ref, wu_ref, wd_ref)
    h1_ref[...] = h1
    u = _rms(h1, gm_ref[...]).astype(_bf16)
    half = HEAD_DIM // 2

    p = jnp.dot(u, wrow_ref[...], preferred_element_type=_f32)
    pk = p[:, :K_COLS]
    lane = lax.broadcasted_iota(_i32, pk.shape, 1)
    swapped = jnp.where((lane & (HEAD_DIM - 1)) < half,
                        pltpu.roll(pk, K_COLS - half, 1), pltpu.roll(pk, half, 1))
    reps = K_COLS // LANES
    cos = jnp.concatenate([cos_ref[...]] * reps, axis=1)
    sin = jnp.concatenate([sin_ref[...]] * reps, axis=1)
    k_ref[...] = (pk * cos + swapped * sin).astype(_bf16)
    gate_ref[...] = jax.nn.sigmoid(p[:, K_COLS:])

    pt = lax.dot_general(wt_ref[...], u, (((1,), (1,)), ((), ())),
                         preferred_element_type=_f32)
    tm = pt.shape[1]
    nq = QT_ROWS // HEAD_DIM
    q3 = pt[:QT_ROWS].reshape(nq, HEAD_DIM, tm)
    sw = jnp.concatenate([q3[:, half:], q3[:, :half]], axis=1)
    q3 = q3 * cost_ref[...][None] + sw * sint_ref[...][None]
    head = lax.broadcasted_iota(_i32, (nq, 1, 1), 0)
    q3 = q3 * jnp.where(head < SA_HEADS + SW_HEADS, LOG2E, 1.0)
    qt_ref[...] = q3.reshape(QT_ROWS, tm).astype(_bf16)
    vrow = lax.broadcasted_iota(_i32, (VT_ROWS, tm), 0)
    ones_row = jnp.zeros((VT_ROWS, tm), _f32)
    for g in range(VT_ROWS // V_GROUP):
        ones_row = jnp.where(vrow == g * V_GROUP + HEAD_DIM, 1.0, ones_row)
    vt_ref[...] = (pt[QT_ROWS:QT_ROWS + VT_ROWS] + ones_row).astype(_bf16)
    wit_ref[...] = pt[QT_ROWS + VT_ROWS:]


def _const_spec(shape):
    nd = len(shape)
    return pl.BlockSpec(shape, lambda *_: (0,) * nd, pipeline_mode=pl.Buffered(1))


def _front(xs, tabs, g1, gm, wg, wu, wd, wrow, wt, *, tm):
    B, S, D = xs.shape
    F = wg.shape[1]
    row_cos, row_sin, col_cos, col_sin = tabs
    gate_cols = wrow.shape[1] - K_COLS
    row_blk = lambda w: pl.BlockSpec((None, tm, w), lambda b, j: (b, j, 0))
    col_blk = lambda r: pl.BlockSpec((None, r, tm), lambda b, j: (b, 0, j))
    out_shape = (
        jax.ShapeDtypeStruct((B, S, D), _f32),
        jax.ShapeDtypeStruct((B, S, K_COLS), _bf16),
        jax.ShapeDtypeStruct((B, S, gate_cols), _f32),
        jax.ShapeDtypeStruct((B, QT_ROWS, S), _bf16),
        jax.ShapeDtypeStruct((B, VT_ROWS, S), _bf16),
        jax.ShapeDtypeStruct((B, WI_ROWS, S), _f32),
    )
    return pl.pallas_call(
        _front_kernel,
        out_shape=out_shape,
        grid=(B, S // tm),
        in_specs=[
            row_blk(D),
            pl.BlockSpec((tm, LANES), lambda b, j: (j, 0)),
            pl.BlockSpec((tm, LANES), lambda b, j: (j, 0)),
            pl.BlockSpec((HEAD_DIM, tm), lambda b, j: (0, j)),
            pl.BlockSpec((HEAD_DIM, tm), lambda b, j: (0, j)),
            _const_spec((1, D)), _const_spec((1, D)),
            _const_spec((D, F)), _const_spec((D, F)), _const_spec((F, D)),
            _const_spec(wrow.shape), _const_spec(wt.shape),
        ],
        out_specs=(row_blk(D), row_blk(K_COLS), row_blk(gate_cols),
                   col_blk(QT_ROWS), col_blk(VT_ROWS), col_blk(WI_ROWS)),
        compiler_params=pltpu.CompilerParams(
            dimension_semantics=("arbitrary", "arbitrary"),
            vmem_limit_bytes=VMEM_LIMIT_BYTES),
        name="front",
    )(xs, row_cos, row_sin, col_cos, col_sin, g1, gm, wg, wu, wd, wrow, wt)


def _sortable(bits):
    return bits ^ (lax.shift_right_arithmetic(bits, 31) & jnp.int32(0x7FFFFFFF))


def _np_key(x):
    bits = np.array(x, np.float32).view(np.int32)
    return bits ^ ((bits >> 31) & np.int32(0x7FFFFFFF))


def _key_to_float(key):
    return lax.bitcast_convert_type(_sortable(key), _f32)


def _float_to_key(x):
    return _sortable(lax.bitcast_convert_type(x, _i32))


def _sublane_all(x, op):
    x = op(x, pltpu.roll(x, 4, 0))
    x = op(x, pltpu.roll(x, 2, 0))
    return op(x, pltpu.roll(x, 1, 0))


def _tree(parts, op):
    while len(parts) > 1:
        parts = [op(parts[j], parts[j + 1]) if j + 1 < len(parts) else parts[j] for j in range(0, len(parts), 2)]
    return parts[0]


def _attn_kernel(sink_ref, qt_ref, wit_ref, k_ref, vt_ref, tri_ref, osp_ref, osw_ref,
                 st_scr, wq_scr, wsw_scr, rawa_scr, rawb_scr, tie_scr, acc_scr, *, topk):
    tq = BLOCK
    i = pl.program_id(1)
    nch = (i + 3) // 2
    nsl = (nch + 1) // 2
    n32 = KEY_CHUNK // SUBLANES
    s32 = SEARCH_CHUNK // SUBLANES
    krow = lax.broadcasted_iota(_i32, (KEY_CHUNK, tq), 0)
    qcol = BLOCK * (i + 1) + lax.broadcasted_iota(_i32, (KEY_CHUNK, tq), 1)

    def chunk_off(c):
        return pl.multiple_of(c * KEY_CHUNK, KEY_CHUNK)

    def slab_off(c):
        return pl.multiple_of(c * SEARCH_CHUNK, SEARCH_CHUNK)

    def pair_cols(pair):
        return slice(2 * pair * tq, 2 * (pair + 1) * tq)

    def store_pair(raw_scr, pair, s2):
        raw_scr[2 * pair] = s2[:, :tq]
        raw_scr[2 * pair + 1] = s2[:, tq:]


    zero_panel = jnp.zeros((HEAD_DIM, tq), _bf16)
    for h in range(IDX_HEADS):
        wq_scr[0:HEAD_DIM, h * tq:(h + 1) * tq] = zero_panel
        wq_scr[HEAD_DIM:, h * tq:(h + 1) * tq] = qt_ref[QI_ROW + h * HEAD_DIM:QI_ROW + (h + 1) * HEAD_DIM, :]
    wi = wit_ref[...]
    wrow = jnp.concatenate([jnp.broadcast_to(wi[h:h + 1], (SUBLANES, tq)) for h in range(IDX_HEADS)], axis=1)

    def index_dots(c, raw_scr):
        kc = k_ref[pl.ds(chunk_off(c), KEY_CHUNK), 0:2 * HEAD_DIM]
        for pair in range(IDX_HEADS // 2):
            store_pair(raw_scr, pair, jnp.dot(kc, wq_scr[:, pair_cols(pair)], preferred_element_type=_f32))

    def index_finish(c, raw_scr, top):
        off = chunk_off(c)
        terms = [jnp.maximum(raw_scr[h], 0.0).reshape(n32, SUBLANES, tq) * wrow[None, :, h * tq:(h + 1) * tq]
                 for h in range(IDX_HEADS)]
        kidx = off + krow
        sc = jnp.where((kidx >= N_PAD) & (kidx <= qcol), _tree(terms, jnp.add).reshape(KEY_CHUNK, tq), NEG_INF)
        st_scr[pl.ds(off, KEY_CHUNK), :] = sc
        sc = sc.reshape(n32, SUBLANES, tq)
        return jnp.maximum(top, _tree([sc[j] for j in range(n32)], jnp.maximum))

    last_chunk = 2 * nsl - 1

    def index_body(j, top):
        index_dots(2 * j + 1, rawb_scr)
        top = index_finish(2 * j, rawa_scr, top)
        index_dots(jnp.minimum(2 * j + 2, last_chunk), rawa_scr)
        return index_finish(2 * j + 1, rawb_scr, top)

    def store_heads(o_ref, outs):
        for j in range(0, len(outs), 2):
            o_ref[:, j * HEAD_DIM:(j + 2) * HEAD_DIM] = jnp.concatenate(outs[j:j + 2], axis=0).T.astype(o_ref.dtype)

    def sliding_window():
        koff = pl.multiple_of(i * BLOCK, BLOCK)
        kidx = koff + krow
        diff = qcol - kidx
        wmask = (kidx >= N_PAD) & (diff >= 0) & (diff < WINDOW)
        outs = []
        for g in range(SW_KV_HEADS):
            for j in range(SW_GROUP):
                hq = g * SW_GROUP + j
                for half in range(SW_KV_HEADS):
                    wsw_scr[half * HEAD_DIM:(half + 1) * HEAD_DIM, hq * tq:(hq + 1) * tq] = (
                        qt_ref[QS_ROW + hq * HEAD_DIM:QS_ROW + (hq + 1) * HEAD_DIM, :] if half == g else zero_panel)
            s = jnp.dot(k_ref[pl.ds(koff, 2 * BLOCK), 2 * HEAD_DIM:4 * HEAD_DIM],
                        wsw_scr[:, g * SW_GROUP * tq:(g + 1) * SW_GROUP * tq],
                        preferred_element_type=_f32)
            es, ms = [], []
            for j in range(SW_GROUP):
                sink = sink_ref[g * SW_GROUP + j] * LOG2E
                sh = jnp.where(wmask, s[:, j * tq:(j + 1) * tq], NEG_INF).reshape(n32, SUBLANES, tq)
                m = jnp.maximum(_sublane_all(_tree([sh[r] for r in range(n32)], jnp.maximum), jnp.maximum), sink)
                es.append(jnp.exp2(sh - m[None]).reshape(2 * BLOCK, tq).astype(_bf16))
                ms.append(jnp.exp2(sink - m[:1]))
            ev = jnp.dot(vt_ref[(1 + g) * V_GROUP:(2 + g) * V_GROUP, pl.ds(koff, 2 * BLOCK)],
                         jnp.concatenate(es, axis=1), preferred_element_type=_f32)
            for j in range(SW_GROUP):
                outs.append(ev[0:HEAD_DIM, j * tq:(j + 1) * tq]
                            / (ev[HEAD_DIM:HEAD_DIM + 1, j * tq:(j + 1) * tq] + ms[j]))
        store_heads(osw_ref, outs)

    index_dots(0, rawa_scr)
    sliding_window()
    top = lax.fori_loop(0, nsl, index_body, jnp.full((SUBLANES, tq), NEG_INF, _f32))

    def count32(*preds):
        def body(c, accs):
            st = st_scr[pl.ds(slab_off(c), SEARCH_CHUNK), :].reshape(s32 // 4, 4, SUBLANES, tq)
            out = []
            for pred, acc in zip(preds, accs):
                ones = jnp.where(pred(st), 1, 0).astype(_i32)
                out.append(acc + _tree([ones[j] for j in range(s32 // 4)], jnp.add))
            return tuple(out)
        accs = lax.fori_loop(0, nsl, body, tuple(jnp.zeros((4, SUBLANES, tq), _i32) for _ in preds))
        return [_sublane_all(_tree([acc[j] for j in range(4)], jnp.add), jnp.add) for acc in accs]

    def count_ge(key):
        t = _key_to_float(key)
        return count32(lambda st: st >= t)[0]

    ub = _sublane_all(top, jnp.maximum)

    def probe(lo, hi, chi, mid):
        mid = jnp.minimum(jnp.maximum(mid, lo), hi - 1)
        cnt = count_ge(mid)
        ge = cnt >= topk
        return (jnp.where(ge, mid, lo), jnp.where(cnt == topk, mid + 1, jnp.where(ge, hi, mid)),
                jnp.where(ge, chi, cnt))

    def key_mid(lo, hi):
        return lax.shift_right_arithmetic(lo, 1) + lax.shift_right_arithmetic(hi, 1) + (lo & hi & 1)

    def value_mid(lo, hi):
        vlo, vhi = _key_to_float(lo), _key_to_float(hi)
        v = jnp.where(vlo == -jnp.inf, 2.0 * vhi - jnp.abs(ub), 0.5 * (vlo + vhi))
        mid = _float_to_key(v)
        return jnp.where((mid > lo) & (mid < hi), mid, key_mid(lo, hi))

    def max_below(t):
        def body(c, acc):
            st = st_scr[pl.ds(slab_off(c), SEARCH_CHUNK), :].reshape(s32 // 4, 4, SUBLANES, tq)
            v = jnp.where(st < t, st, -jnp.inf)
            return jnp.maximum(acc, _tree([v[j] for j in range(s32 // 4)], jnp.maximum))
        acc = lax.fori_loop(0, nsl, body, jnp.full((4, SUBLANES, tq), -jnp.inf, _f32))
        return _sublane_all(_tree([acc[j] for j in range(4)], jnp.maximum), jnp.maximum)

    def unsettled(lo, hi):
        return jnp.max(jnp.where(hi - lo != 1, 1, 0).astype(_i32))

    c_ge0, c_gt0, c_adm = count32(lambda st: st >= 0.0, lambda st: st > 0.0, lambda st: st >= SEL_FLOOR)
    nonneg = c_ge0 >= topk
    at_zero = (c_gt0 < topk) & nonneg
    take_all = c_adm <= topk
    key_neg = int(_np_key(NEG_INF))
    lo = jnp.where(take_all, key_neg, jnp.where(nonneg, 0, KEY_BELOW_ALL))
    hi = jnp.where(take_all, key_neg + 1, jnp.where(at_zero, 1, jnp.where(nonneg, KEY_ABOVE_ALL, 0)))
    chi = jnp.where(nonneg, 0, c_ge0)
    lo, hi, chi = probe(lo, hi, chi, _float_to_key(ub) + 1)
    for _ in range(VALUE_BISECT_STEPS):
        lo, hi, chi = probe(lo, hi, chi, value_mid(lo, hi))

    settled = hi - lo == 1
    left = jnp.where(settled, 0, topk - chi)
    t8 = jnp.where(settled, _key_to_float(lo), _key_to_float(hi))
    for _ in range(EXTRACT_PASSES):
        below = max_below(t8)
        t8 = jnp.where(left > 0, below, t8)
        left = jnp.maximum(left - 1, 0)
    cnt_gt, cnt_ge = count32(lambda st: st > t8, lambda st: st >= t8)
    good = take_all | (settled | (left == 0)) & (cnt_ge >= topk) & ((cnt_gt < topk) | (cnt_ge == topk))

    def tie_flags(t, gt, ge):
        return (ge - gt > topk - gt) & (t >= SEL_FLOOR)

    hi = jnp.where(good, lo + 1, hi)
    events = jnp.max(jnp.where(hi - lo != 1, 2, 0) | jnp.where(tie_flags(t8, cnt_gt, cnt_ge), 1, 0))

    def bisect_rest():
        def search_body(state):
            lo, hi, chi, _ = state
            for _ in range(BISECT_STEPS_PER_CHECK):
                lo, hi, chi = probe(lo, hi, chi, key_mid(lo, hi))
            return lo, hi, chi, unsettled(lo, hi)
        lo2 = lax.while_loop(lambda state: state[3] > 0, search_body, (lo, hi, chi, jnp.int32(1)))[0]
        t = jnp.where(good, t8, _key_to_float(lo2))
        gt, ge = count32(lambda st: st > t, lambda st: st >= t)
        return t, gt, ge, jnp.max(jnp.where(tie_flags(t, gt, ge), 1, 0).astype(_i32))

    t8, cnt_gt, cnt_ge, has_ties = lax.cond(events >= 2, bisect_rest, lambda: (t8, cnt_gt, cnt_ge, events & 1))
    need = topk - cnt_gt

    for h in range(SA_HEADS):
        wq_scr[0:HEAD_DIM, h * tq:(h + 1) * tq] = qt_ref[h * HEAD_DIM:(h + 1) * HEAD_DIM, :]
        wq_scr[HEAD_DIM:, h * tq:(h + 1) * tq] = zero_panel
    acc_scr[...] = jnp.zeros(acc_scr.shape, _f32)

    def score_dots(c, raw_scr):
        kc = k_ref[pl.ds(chunk_off(c), KEY_CHUNK), 0:2 * HEAD_DIM]
        for pair in range(SA_HEADS // 2):
            store_pair(raw_scr, pair, jnp.dot(kc, wq_scr[:, pair_cols(pair)], preferred_element_type=_f32))

    score_dots(0, rawa_scr)

    @pl.when(has_ties > 0)
    def _():
        need_f = need.astype(_f32)

        def tie_dots(j, slot):
            ind = [jnp.where(st_scr[pl.ds(chunk_off(2 * j + e), KEY_CHUNK), :].reshape(n32, SUBLANES, tq) == t8[None],
                             1.0, 0.0).reshape(KEY_CHUNK, tq).astype(_bf16) for e in range(2)]
            tie_scr[slot] = jnp.dot(tri_ref[...], jnp.concatenate(ind, axis=1), preferred_element_type=_f32)

        def tie_finish(j, slot, before):
            both = tie_scr[slot]
            for e in range(2):
                off = chunk_off(2 * j + e)
                st = st_scr[pl.ds(off, KEY_CHUNK), :].reshape(n32, SUBLANES, tq)
                within = both[:, e * tq:(e + 1) * tq].reshape(n32, SUBLANES, tq)
                drop = (st == t8[None]) & (within + before[None] > need_f[None])
                st_scr[pl.ds(off, KEY_CHUNK), :] = jnp.where(drop, NEG_INF, st).reshape(KEY_CHUNK, tq)
                before = before + jnp.broadcast_to(within[n32 - 1, SUBLANES - 1:, :], (SUBLANES, tq))
            return before

        def tie_body(j, before):
            before = tie_finish(j, j % 2, before)
            tie_dots(jnp.minimum(j + 1, nsl - 1), (j + 1) % 2)
            return before

        tie_dots(0, 0)
        lax.fori_loop(0, nsl, tie_body, jnp.zeros((SUBLANES, tq), _f32))

    tsel = jnp.maximum(t8, SEL_FLOOR)

    def chunk_update(c, raw_scr, m_old):
        off = chunk_off(c)
        sel = st_scr[pl.ds(off, KEY_CHUNK), :].reshape(n32, SUBLANES, tq) >= tsel[None]
        vc = vt_ref[0:V_GROUP, pl.ds(off, KEY_CHUNK)]
        m_new = []
        for pair in range(SA_HEADS // 2):
            ps, alphas = [], []
            for h in (2 * pair, 2 * pair + 1):
                sh = jnp.where(sel, raw_scr[h].reshape(n32, SUBLANES, tq), NEG_INF)
                mh = jnp.maximum(m_old[h], _sublane_all(_tree([sh[j] for j in range(n32)], jnp.maximum), jnp.maximum))
                ps.append(jnp.exp2(sh - mh[None]).reshape(KEY_CHUNK, tq).astype(_bf16))
                alphas.append(jnp.exp2(m_old[h] - mh)[:1])
                m_new.append(mh)
            acc_scr[:, pair_cols(pair)] = (jnp.concatenate(alphas, axis=1) * acc_scr[:, pair_cols(pair)]
                                           + jnp.dot(vc, jnp.concatenate(ps, axis=1), preferred_element_type=_f32))
        return m_new

    def attn_body(j, m):
        score_dots(2 * j + 1, rawb_scr)
        m = chunk_update(2 * j, rawa_scr, list(m))
        m = chunk_update(2 * j + 1, rawb_scr, m)
        score_dots(jnp.minimum(2 * j + 2, last_chunk), rawa_scr)
        return tuple(m)

    lax.fori_loop(0, nsl, attn_body, tuple(jnp.full((SUBLANES, tq), NEG_INF, _f32) for _ in range(SA_HEADS)))

    acc = acc_scr[...]
    store_heads(osp_ref, [acc[0:HEAD_DIM, h * tq:(h + 1) * tq] / acc[HEAD_DIM:HEAD_DIM + 1, h * tq:(h + 1) * tq]
                          for h in range(SA_HEADS)])


def _attention(sinks, qt, wit, k, vt, *, topk):
    B, _, S = qt.shape
    nkp = k.shape[1]
    tq = BLOCK
    kernel = functools.partial(_attn_kernel, topk=topk)
    out_blk = pl.BlockSpec((None, tq, SA_HEADS * HEAD_DIM), lambda b, i: (b, i, 0))
    return pl.pallas_call(
        kernel,
        out_shape=(jax.ShapeDtypeStruct((B, S, SA_HEADS * HEAD_DIM), _bf16),
                   jax.ShapeDtypeStruct((B, S, SW_HEADS * HEAD_DIM), _bf16)),
        grid=(B, S // tq),
        in_specs=[
            pl.BlockSpec(memory_space=pltpu.SMEM),
            pl.BlockSpec((None, QT_ROWS, tq), lambda b, i: (b, 0, i)),
            pl.BlockSpec((None, WI_ROWS, tq), lambda b, i: (b, 0, i)),
            pl.BlockSpec((None, nkp, K_COLS), lambda b, i: (b, 0, 0)),
            pl.BlockSpec((None, VT_ROWS, nkp), lambda b, i: (b, 0, 0)),
            _const_spec((KEY_CHUNK, KEY_CHUNK)),
        ],
        out_specs=(out_blk, out_blk),
        scratch_shapes=[
            pltpu.VMEM((nkp, tq), _f32),
            pltpu.VMEM((2 * HEAD_DIM, SA_HEADS * tq), _bf16),
            pltpu.VMEM((2 * HEAD_DIM, SW_HEADS * tq), _bf16),
            pltpu.VMEM((SA_HEADS, KEY_CHUNK, tq), _f32),
            pltpu.VMEM((SA_HEADS, KEY_CHUNK, tq), _f32),
            pltpu.VMEM((2, KEY_CHUNK, 2 * tq), _f32),
            pltpu.VMEM((V_GROUP, SA_HEADS * tq), _f32),
        ],
        compiler_params=pltpu.CompilerParams(
            dimension_semantics=("arbitrary", "arbitrary"),
            vmem_limit_bytes=VMEM_LIMIT_BYTES),
        name="attn",
    )(sinks, qt, wit, k, vt, jnp.tril(jnp.ones((KEY_CHUNK, KEY_CHUNK), _bf16)))


def _back_kernel(h1_ref, osp_ref, osw_ref, gate_ref, wbs_ref, wbw_ref, wo_ref, g2_ref,
                 wg_ref, wu_ref, wd_ref, gf_ref, out_ref):
    d = h1_ref.shape[-1]
    a = jnp.dot(osp_ref[...], wbs_ref[...], preferred_element_type=_f32)
    b = jnp.dot(osw_ref[...], wbw_ref[...], preferred_element_type=_f32)
    gates = gate_ref[...]
    merged = (gates[:, :d] * a + gates[:, d:] * b).astype(_bf16)
    h2 = h1_ref[...] + jnp.dot(merged, wo_ref[...], preferred_element_type=_f32)
    h3 = _swiglu_half_step(h2, g2_ref[...], wg_ref, wu_ref, wd_ref)
    out_ref[...] = _rms(h3, gf_ref[...])


def _back(h1, osp, osw, gates, wbs, wbw, wo, g2, wg, wu, wd, gf, *, tm):
    B, S, D = h1.shape
    F = wg.shape[1]
    row_blk = lambda w: pl.BlockSpec((None, tm, w), lambda b, j: (b, j, 0))
    return pl.pallas_call(
        _back_kernel,
        out_shape=jax.ShapeDtypeStruct((B, S, D), _f32),
        grid=(B, S // tm),
        in_specs=[
            row_blk(D), row_blk(osp.shape[-1]), row_blk(osw.shape[-1]), row_blk(gates.shape[-1]),
            _const_spec(wbs.shape), _const_spec(wbw.shape), _const_spec(wo.shape), _const_spec((1, D)),
            _const_spec((D, F)), _const_spec((D, F)), _const_spec((F, D)), _const_spec((1, D)),
        ],
        out_specs=row_blk(D),
        compiler_params=pltpu.CompilerParams(
            dimension_semantics=("arbitrary", "arbitrary"),
            vmem_limit_bytes=VMEM_LIMIT_BYTES),
        name="back",
    )(h1, osp, osw, gates, wbs, wbw, wo, g2, wg, wu, wd, gf)


def _rope_tables(rpos):
    inv_freq = 1.0 / (ROPE_THETA ** (jnp.arange(0, HEAD_DIM, 2, dtype=_f32) / HEAD_DIM))
    ang = rpos.astype(_f32)[:, None] * inv_freq[None, :]
    cos, sin = jnp.cos(ang), jnp.sin(ang)
    cos64 = jnp.concatenate([cos, cos], axis=1)
    sin64 = jnp.concatenate([-sin, sin], axis=1)
    reps = LANES // HEAD_DIM
    return (jnp.tile(cos64, (1, reps)), jnp.tile(sin64, (1, reps)), cos64.T, sin64.T)


def _pack_w_in(w):
    sizes = (SA_HEADS * HEAD_DIM, HEAD_DIM, HEAD_DIM, IDX_HEADS * IDX_DIM, IDX_DIM, IDX_HEADS,
             SW_HEADS * HEAD_DIM, SW_KV_HEADS * HEAD_DIM, SW_KV_HEADS * HEAD_DIM)
    d = w.shape[0]
    gate_cols = w.shape[1] - sum(sizes)
    splits = [int(s) for s in np.cumsum(sizes + (gate_cols,))[:-1]]
    qa, ka, va, qi, ki, wi, qs, ksw, vsw, gates = jnp.split(w, splits, axis=1)
    scale = HEAD_DIM ** -0.5
    idx_scale = (IDX_HEADS ** -0.5) * (IDX_DIM ** -0.5)
    zeros = lambda n: jnp.zeros((d, n), w.dtype)
    wrow = jnp.concatenate([ka, ki, ksw, gates], axis=1)
    pad = zeros(V_GROUP - HEAD_DIM)
    wt = jnp.concatenate(
        [qa * scale, qs * scale, qi, va, pad]
        + [blk for g in range(SW_KV_HEADS) for blk in (vsw[:, g * HEAD_DIM:(g + 1) * HEAD_DIM], pad)]
        + [wi * idx_scale, zeros(WI_ROWS - wi.shape[1])], axis=1).T
    return wrow.astype(_bf16), wt.astype(_bf16)


def kernel(x, meta_tokens, norm_ffn1, w_ffn1_gate, w_ffn1_up, w_ffn1_down, norm_mix, w_in, sinks,
           w_branch_sparse, w_branch_swa, w_out, norm_ffn2, w_ffn2_gate, w_ffn2_up, w_ffn2_down, norm_final):
    B, S, D = x.shape
    assert norm_ffn1.shape[0] == 1, "single-layer stack only"
    assert S % BLOCK == 0
    topk = min(TOPK_MAX, S // 4)
    tm = 256 if S % 256 == 0 else BLOCK
    bf = lambda a: a.astype(_bf16)

    wrow, wt = _pack_w_in(w_in[0])
    g1, gm, g2 = norm_ffn1[0][None], norm_mix[0][None], norm_ffn2[0][None]
    wg1, wu1, wd1 = bf(w_ffn1_gate[0]), bf(w_ffn1_up[0]), bf(w_ffn1_down[0])

    pos_real = N_META + jnp.arange(S, dtype=jnp.int32)
    pos_meta = jnp.maximum(jnp.arange(BLOCK, dtype=jnp.int32) - N_PAD, 0)
    h1, k, gates, qt, vt, wit = _front(x, _rope_tables(pos_real), g1, gm, wg1, wu1, wd1, wrow, wt, tm=tm)
    meta_blk = jnp.concatenate([jnp.zeros((N_PAD, D), x.dtype), meta_tokens.astype(x.dtype)], axis=0)[None]
    _, k_m, _, _, vt_m, _ = _front(meta_blk, _rope_tables(pos_meta), g1, gm, wg1, wu1, wd1, wrow, wt, tm=BLOCK)

    nkp = -(-(S + BLOCK) // SEARCH_CHUNK) * SEARCH_CHUNK
    k_all = jnp.concatenate([jnp.broadcast_to(k_m, (B, BLOCK, K_COLS)), k,
                             jnp.zeros((B, nkp - S - BLOCK, K_COLS), k.dtype)], axis=1)
    vt_all = jnp.concatenate([jnp.broadcast_to(vt_m, (B, VT_ROWS, BLOCK)), vt,
                              jnp.zeros((B, VT_ROWS, nkp - S - BLOCK), vt.dtype)], axis=2)

    o_sparse, o_swa = _attention(sinks[0].astype(_f32), qt, wit, k_all, vt_all, topk=topk)

    return _back(h1, o_sparse, o_swa, gates, bf(w_branch_sparse[0]), bf(w_branch_swa[0]), bf(w_out[0]), g2,
                 bf(w_ffn2_gate[0]), bf(w_ffn2_up[0]), bf(w_ffn2_down[0]), norm_final[None], tm=tm)
```

```python
import functools

import numpy as np
import jax
import jax.numpy as jnp
from jax import lax
from jax.experimental import pallas as pl
from jax.experimental.pallas import tpu as pltpu

N_META = 16
BLOCK = 128
HEAD_DIM = 64
ROPE_THETA = 10000.0
EPS = 1e-6
NEG_INF = -1e30
SA_HEADS = 8
TOPK_MAX = 256
IDX_HEADS = 4
IDX_DIM = 64
SW_HEADS = 8
SW_KV_HEADS = 2
WINDOW = 128
N_PAD = BLOCK - N_META
SW_GROUP = SW_HEADS // SW_KV_HEADS
LOG2E = 1.4426950408889634

LANES = 128
SUBLANES = 8
PACK16 = 16
KEY_CHUNK = 256
SEARCH_CHUNK = 2 * KEY_CHUNK
VMEM_LIMIT_BYTES = 56 * 1024 * 1024

QT_ROWS = (SA_HEADS + SW_HEADS + IDX_HEADS) * HEAD_DIM
QS_ROW = SA_HEADS * HEAD_DIM
QI_ROW = QS_ROW + SW_HEADS * HEAD_DIM
V_GROUP = HEAD_DIM + PACK16
VT_ROWS = (1 + SW_KV_HEADS) * V_GROUP
WI_ROWS = PACK16
PT_ROWS = QT_ROWS + VT_ROWS + WI_ROWS
K_COLS = 4 * HEAD_DIM
SEL_FLOOR = float(np.nextafter(np.float32(0.5 * NEG_INF), np.float32(0.0)))
KEY_BELOW_ALL = int(np.int32(np.uint32(0xFF800000) ^ np.uint32(0x7FFFFFFF)))
KEY_ABOVE_ALL = 0x7F800001
VALUE_BISECT_STEPS = 14
EXTRACT_PASSES = 3
BISECT_STEPS_PER_CHECK = 3
ROW_SPLITS = 2

_f32 = jnp.float32
_bf16 = jnp.bfloat16
_i32 = jnp.int32


def _rms(x, g):
    return x * lax.rsqrt(jnp.mean(x * x, axis=-1, keepdims=True) + EPS) * g


def _swiglu_half_step(x, g_norm, wg_ref, wu_ref, wd_ref):
    a = _rms(x, g_norm).astype(_bf16)
    g = jnp.dot(a, wg_ref[...], preferred_element_type=_f32)
    u = jnp.dot(a, wu_ref[...], preferred_element_type=_f32)
    act = (g * jax.nn.sigmoid(g) * u).astype(_bf16)
    return x + 0.5 * jnp.dot(act, wd_ref[...], preferred_element_type=_f32)


def _front_kernel(x_ref, cos_ref, sin_ref, cost_ref, sint_ref, g1_ref, gm_ref,
                  wg_ref, wu_ref, wd_ref, wrow_ref, wt_ref,
                  h1_ref, k_ref, gate_ref, qt_ref, vt_ref, wit_ref):
    tm = x_ref.shape[0]
    hm = tm // _row_splits(tm)
    for r in range(tm // hm):
        _front_rows(slice(r * hm, (r + 1) * hm), x_ref, cos_ref, sin_ref, cost_ref, sint_ref, g1_ref, gm_ref,
                    wg_ref, wu_ref, wd_ref, wrow_ref, wt_ref, h1_ref, k_ref, gate_ref, qt_ref, vt_ref, wit_ref)


def _row_splits(tm):
    return ROW_SPLITS if tm % (ROW_SPLITS * 2 * LANES) == 0 else 1


def _front_rows(rows, x_ref, cos_ref, sin_ref, cost_ref, sint_ref, g1_ref, gm_ref,
                wg_ref, wu_ref, wd_ref, wrow_ref, wt_ref,
                h1_ref, k_ref, gate_ref, qt_ref, vt_ref, wit_ref):
    h1 = _swiglu_half_step(x_ref[rows, :], g1_ref[...], wg_ref, wu_ref, wd_ref)
    h1_ref[rows, :] = h1
    u = _rms(h1, gm_ref[...]).astype(_bf16)
    half = HEAD_DIM // 2

    p = jnp.dot(u, wrow_ref[...], preferred_element_type=_f32)
    pk = p[:, :K_COLS]
    lane = lax.broadcasted_iota(_i32, pk.shape, 1)
    swapped = jnp.where((lane & (HEAD_DIM - 1)) < half,
                        pltpu.roll(pk, K_COLS - half, 1), pltpu.roll(pk, half, 1))
    reps = K_COLS // LANES
    cos = jnp.concatenate([cos_ref[rows, :]] * reps, axis=1)
    sin = jnp.concatenate([sin_ref[rows, :]] * reps, axis=1)
    k_ref[rows, :] = (pk * cos + swapped * sin).astype(_bf16)
    gate_ref[rows, :] = jax.nn.sigmoid(p[:, K_COLS:])

    pt = lax.dot_general(wt_ref[...], u, (((1,), (1,)), ((), ())),
                         preferred_element_type=_f32)
    hm = pt.shape[1]
    nq = QT_ROWS // HEAD_DIM
    q3 = pt[:QT_ROWS].reshape(nq, HEAD_DIM, hm)
    sw = jnp.concatenate([q3[:, half:], q3[:, :half]], axis=1)
    q3 = q3 * cost_ref[:, rows][None] + sw * sint_ref[:, rows][None]
    head = lax.broadcasted_iota(_i32, (nq, 1, 1), 0)
    q3 = q3 * jnp.where(head < SA_HEADS + SW_HEADS, LOG2E, 1.0)
    qt_ref[:, rows] = q3.reshape(QT_ROWS, hm).astype(_bf16)
    vrow = lax.broadcasted_iota(_i32, (VT_ROWS, hm), 0)
    ones_row = jnp.zeros((VT_ROWS, hm), _f32)
    for g in range(VT_ROWS // V_GROUP):
        ones_row = jnp.where(vrow == g * V_GROUP + HEAD_DIM, 1.0, ones_row)
    vt_ref[:, rows] = (pt[QT_ROWS:QT_ROWS + VT_ROWS] + ones_row).astype(_bf16)
    wit_ref[:, rows] = pt[QT_ROWS + VT_ROWS:]


def _const_spec(shape):
    nd = len(shape)
    return pl.BlockSpec(shape, lambda *_: (0,) * nd, pipeline_mode=pl.Buffered(1))


def _front(xs, tabs, g1, gm, wg, wu, wd, wrow, wt, *, tm):
    B, S, D = xs.shape
    F = wg.shape[1]
    row_cos, row_sin, col_cos, col_sin = tabs
    gate_cols = wrow.shape[1] - K_COLS
    row_blk = lambda w: pl.BlockSpec((None, tm, w), lambda b, j: (b, j, 0))
    col_blk = lambda r: pl.BlockSpec((None, r, tm), lambda b, j: (b, 0, j))
    out_shape = (
        jax.ShapeDtypeStruct((B, S, D), _f32),
        jax.ShapeDtypeStruct((B, S, K_COLS), _bf16),
        jax.ShapeDtypeStruct((B, S, gate_cols), _f32),
        jax.ShapeDtypeStruct((B, QT_ROWS, S), _bf16),
        jax.ShapeDtypeStruct((B, VT_ROWS, S), _bf16),
        jax.ShapeDtypeStruct((B, WI_ROWS, S), _f32),
    )
    return pl.pallas_call(
        _front_kernel,
        out_shape=out_shape,
        grid=(B, S // tm),
        in_specs=[
            row_blk(D),
            pl.BlockSpec((tm, LANES), lambda b, j: (j, 0)),
            pl.BlockSpec((tm, LANES), lambda b, j: (j, 0)),
            pl.BlockSpec((HEAD_DIM, tm), lambda b, j: (0, j)),
            pl.BlockSpec((HEAD_DIM, tm), lambda b, j: (0, j)),
            _const_spec((1, D)), _const_spec((1, D)),
            _const_spec((D, F)), _const_spec((D, F)), _const_spec((F, D)),
            _const_spec(wrow.shape), _const_spec(wt.shape),
        ],
        out_specs=(row_blk(D), row_blk(K_COLS), row_blk(gate_cols),
                   col_blk(QT_ROWS), col_blk(VT_ROWS), col_blk(WI_ROWS)),
        compiler_params=pltpu.CompilerParams(
            dimension_semantics=("arbitrary", "arbitrary"),
            vmem_limit_bytes=VMEM_LIMIT_BYTES),
        name="front",
    )(xs, row_cos, row_sin, col_cos, col_sin, g1, gm, wg, wu, wd, wrow, wt)


def _sortable(bits):
    return bits ^ (lax.shift_right_arithmetic(bits, 31) & jnp.int32(0x7FFFFFFF))


def _np_key(x):
    bits = np.array(x, np.float32).view(np.int32)
    return bits ^ ((bits >> 31) & np.int32(0x7FFFFFFF))


def _key_to_float(key):
    return lax.bitcast_convert_type(_sortable(key), _f32)


def _float_to_key(x):
    return _sortable(lax.bitcast_convert_type(x, _i32))


def _sublane_all(x, op):
    x = op(x, pltpu.roll(x, 4, 0))
    x = op(x, pltpu.roll(x, 2, 0))
    return op(x, pltpu.roll(x, 1, 0))


def _tree(parts, op):
    while len(parts) > 1:
        parts = [op(parts[j], parts[j + 1]) if j + 1 < len(parts) else parts[j] for j in range(0, len(parts), 2)]
    return parts[0]


def _attn_kernel(sink_ref, qt_ref, wit_ref, k_ref, vt_ref, tri_ref, osp_ref, osw_ref,
                 st_scr, wq_scr, wsw_scr, rawa_scr, rawb_scr, tie_scr, acc_scr, *, topk):
    tq = BLOCK
    i = pl.program_id(1)
    nch = (i + 3) // 2
    nsl = (nch + 1) // 2
    n32 = KEY_CHUNK // SUBLANES
    s32 = SEARCH_CHUNK // SUBLANES
    krow = lax.broadcasted_iota(_i32, (KEY_CHUNK, tq), 0)
    qcol = BLOCK * (i + 1) + lax.broadcasted_iota(_i32, (KEY_CHUNK, tq), 1)

    def chunk_off(c):
        return pl.multiple_of(c * KEY_CHUNK, KEY_CHUNK)

    def slab_off(c):
        return pl.multiple_of(c * SEARCH_CHUNK, SEARCH_CHUNK)

    def pair_cols(pair):
        return slice(2 * pair * tq, 2 * (pair + 1) * tq)

    def store_pair(raw_scr, pair, s2):
        raw_scr[2 * pair] = s2[:, :tq]
        raw_scr[2 * pair + 1] = s2[:, tq:]


    zero_panel = jnp.zeros((HEAD_DIM, tq), _bf16)
    for h in range(IDX_HEADS):
        wq_scr[0:HEAD_DIM, h * tq:(h + 1) * tq] = zero_panel
        wq_scr[HEAD_DIM:, h * tq:(h + 1) * tq] = qt_ref[QI_ROW + h * HEAD_DIM:QI_ROW + (h + 1) * HEAD_DIM, :]
    wi = wit_ref[...]
    wrow = jnp.concatenate([jnp.broadcast_to(wi[h:h + 1], (SUBLANES, tq)) for h in range(IDX_HEADS)], axis=1)

    def index_dots(c, raw_scr):
        kc = k_ref[pl.ds(chunk_off(c), KEY_CHUNK), 0:2 * HEAD_DIM]
        for pair in range(IDX_HEADS // 2):
            store_pair(raw_scr, pair, jnp.dot(kc, wq_scr[:, pair_cols(pair)], preferred_element_type=_f32))

    def index_finish(c, raw_scr, top):
        off = chunk_off(c)
        terms = [jnp.maximum(raw_scr[h], 0.0).reshape(n32, SUBLANES, tq) * wrow[None, :, h * tq:(h + 1) * tq]
                 for h in range(IDX_HEADS)]
        kidx = off + krow
        sc = jnp.where((kidx >= N_PAD) & (kidx <= qcol), _tree(terms, jnp.add).reshape(KEY_CHUNK, tq), NEG_INF)
        st_scr[pl.ds(off, KEY_CHUNK), :] = sc
        sc = sc.reshape(n32, SUBLANES, tq)
        return jnp.maximum(top, _tree([sc[j] for j in range(n32)], jnp.maximum))

    last_chunk = 2 * nsl - 1

    def index_body(j, top):
        index_dots(2 * j + 1, rawb_scr)
        top = index_finish(2 * j, rawa_scr, top)
        index_dots(jnp.minimum(2 * j + 2, last_chunk), rawa_scr)
        return index_finish(2 * j + 1, rawb_scr, top)

    def store_heads(o_ref, outs):
        for j in range(0, len(outs), 2):
            o_ref[:, j * HEAD_DIM:(j + 2) * HEAD_DIM] = jnp.concatenate(outs[j:j + 2], axis=0).T.astype(o_ref.dtype)

    def sliding_window():
        koff = pl.multiple_of(i * BLOCK, BLOCK)
        kidx = koff + krow
        diff = qcol - kidx
        wmask = (kidx >= N_PAD) & (diff >= 0) & (diff < WINDOW)
        outs = []
        for g in range(SW_KV_HEADS):
            for j in range(SW_GROUP):
                hq = g * SW_GROUP + j
                for half in range(SW_KV_HEADS):
                    wsw_scr[half * HEAD_DIM:(half + 1) * HEAD_DIM, hq * tq:(hq + 1) * tq] = (
                        qt_ref[QS_ROW + hq * HEAD_DIM:QS_ROW + (hq + 1) * HEAD_DIM, :] if half == g else zero_panel)
            s = jnp.dot(k_ref[pl.ds(koff, 2 * BLOCK), 2 * HEAD_DIM:4 * HEAD_DIM],
                        wsw_scr[:, g * SW_GROUP * tq:(g + 1) * SW_GROUP * tq],
                        preferred_element_type=_f32)
            es, ms = [], []
            for j in range(SW_GROUP):
                sink = sink_ref[g * SW_GROUP + j] * LOG2E
                sh = jnp.where(wmask, s[:, j * tq:(j + 1) * tq], NEG_INF).reshape(n32, SUBLANES, tq)
                m = jnp.maximum(_sublane_all(_tree([sh[r] for r in range(n32)], jnp.maximum), jnp.maximum), sink)
                es.append(jnp.exp2(sh - m[None]).reshape(2 * BLOCK, tq).astype(_bf16))
                ms.append(jnp.exp2(sink - m[:1]))
            ev = jnp.dot(vt_ref[(1 + g) * V_GROUP:(2 + g) * V_GROUP, pl.ds(koff, 2 * BLOCK)],
                         jnp.concatenate(es, axis=1), preferred_element_type=_f32)
            for j in range(SW_GROUP):
                outs.append(ev[0:HEAD_DIM, j * tq:(j + 1) * tq]
                            / (ev[HEAD_DIM:HEAD_DIM + 1, j * tq:(j + 1) * tq] + ms[j]))
        store_heads(osw_ref, outs)

    index_dots(0, rawa_scr)
    sliding_window()
    top = lax.fori_loop(0, nsl, index_body, jnp.full((SUBLANES, tq), NEG_INF, _f32))

    def count32(*preds):
        def body(c, accs):
            st = st_scr[pl.ds(slab_off(c), SEARCH_CHUNK), :].reshape(s32 // 4, 4, SUBLANES, tq)
            out = []
            for pred, acc in zip(preds, accs):
                ones = jnp.where(pred(st), 1, 0).astype(_i32)
                out.append(acc + _tree([ones[j] for j in range(s32 // 4)], jnp.add))
            return tuple(out)
        accs = lax.fori_loop(0, nsl, body, tuple(jnp.zeros((4, SUBLANES, tq), _i32) for _ in preds))
        return [_sublane_all(_tree([acc[j] for j in range(4)], jnp.add), jnp.add) for acc in accs]

    def count_ge(key):
        t = _key_to_float(key)
        return count32(lambda st: st >= t)[0]

    ub = _sublane_all(top, jnp.maximum)

    def probe(lo, hi, chi, mid):
        mid = jnp.minimum(jnp.maximum(mid, lo), hi - 1)
        cnt = count_ge(mid)
        ge = cnt >= topk
        return (jnp.where(ge, mid, lo), jnp.where(cnt == topk, mid + 1, jnp.where(ge, hi, mid)),
                jnp.where(ge, chi, cnt))

    def key_mid(lo, hi):
        return lax.shift_right_arithmetic(lo, 1) + lax.shift_right_arithmetic(hi, 1) + (lo & hi & 1)

    def value_mid(lo, hi):
        vlo, vhi = _key_to_float(lo), _key_to_float(hi)
        v = jnp.where(vlo == -jnp.inf, 2.0 * vhi - jnp.abs(ub), 0.5 * (vlo + vhi))
        mid = _float_to_key(v)
        return jnp.where((mid > lo) & (mid < hi), mid, key_mid(lo, hi))

    def max_below(t):
        def body(c, acc):
            st = st_scr[pl.ds(slab_off(c), SEARCH_CHUNK), :].reshape(s32 // 4, 4, SUBLANES, tq)
            v = jnp.where(st < t, st, -jnp.inf)
            return jnp.maximum(acc, _tree([v[j] for j in range(s32 // 4)], jnp.maximum))
        acc = lax.fori_loop(0, nsl, body, jnp.full((4, SUBLANES, tq), -jnp.inf, _f32))
        return _sublane_all(_tree([acc[j] for j in range(4)], jnp.maximum), jnp.maximum)

    def unsettled(lo, hi):
        return jnp.max(jnp.where(hi - lo != 1, 1, 0).astype(_i32))

    c_ge0, c_gt0, c_adm = count32(lambda st: st >= 0.0, lambda st: st > 0.0, lambda st: st >= SEL_FLOOR)
    nonneg = c_ge0 >= topk
    at_zero = (c_gt0 < topk) & nonneg
    take_all = c_adm <= topk
    key_neg = int(_np_key(NEG_INF))
    lo = jnp.where(take_all, key_neg, jnp.where(nonneg, 0, KEY_BELOW_ALL))
    hi = jnp.where(take_all, key_neg + 1, jnp.where(at_zero, 1, jnp.where(nonneg, KEY_ABOVE_ALL, 0)))
    chi = jnp.where(nonneg, 0, c_ge0)
    lo, hi, chi = probe(lo, hi, chi, _float_to_key(ub) + 1)
    for _ in range(VALUE_BISECT_STEPS):
        lo, hi, chi = probe(lo, hi, chi, value_mid(lo, hi))

    settled = hi - lo == 1
    left = jnp.where(settled, 0, topk - chi)
    t8 = jnp.where(settled, _key_to_float(lo), _key_to_float(hi))
    for _ in range(EXTRACT_PASSES):
        below = max_below(t8)
        t8 = jnp.where(left > 0, below, t8)
        left = jnp.maximum(left - 1, 0)
    cnt_gt, cnt_ge = count32(lambda st: st > t8, lambda st: st >= t8)
    good = take_all | (settled | (left == 0)) & (cnt_ge >= topk) & ((cnt_gt < topk) | (cnt_ge == topk))

    def tie_flags(t, gt, ge):
        return (ge - gt > topk - gt) & (t >= SEL_FLOOR)

    hi = jnp.where(good, lo + 1, hi)
    events = jnp.max(jnp.where(hi - lo != 1, 2, 0) | jnp.where(tie_flags(t8, cnt_gt, cnt_ge), 1, 0))

    def bisect_rest():
        def search_body(state):
            lo, hi, chi, _ = state
            for _ in range(BISECT_STEPS_PER_CHECK):
                lo, hi, chi = probe(lo, hi, chi, key_mid(lo, hi))
            return lo, hi, chi, unsettled(lo, hi)
        lo2 = lax.while_loop(lambda state: state[3] > 0, search_body, (lo, hi, chi, jnp.int32(1)))[0]
        t = jnp.where(good, t8, _key_to_float(lo2))
        gt, ge = count32(lambda st: st > t, lambda st: st >= t)
        return t, gt, ge, jnp.max(jnp.where(tie_flags(t, gt, ge), 1, 0).astype(_i32))

    t8, cnt_gt, cnt_ge, has_ties = lax.cond(events >= 2, bisect_rest, lambda: (t8, cnt_gt, cnt_ge, events & 1))
    need = topk - cnt_gt

    for h in range(SA_HEADS):
        wq_scr[0:HEAD_DIM, h * tq:(h + 1) * tq] = qt_ref[h * HEAD_DIM:(h + 1) * HEAD_DIM, :]
        wq_scr[HEAD_DIM:, h * tq:(h + 1) * tq] = zero_panel
    acc_scr[...] = jnp.zeros(acc_scr.shape, _f32)

    def score_dots(c, raw_scr):
        kc = k_ref[pl.ds(chunk_off(c), KEY_CHUNK), 0:2 * HEAD_DIM]
        for pair in range(SA_HEADS // 2):
            store_pair(raw_scr, pair, jnp.dot(kc, wq_scr[:, pair_cols(pair)], preferred_element_type=_f32))

    score_dots(0, rawa_scr)

    @pl.when(has_ties > 0)
    def _():
        need_f = need.astype(_f32)

        def tie_dots(j, slot):
            ind = [jnp.where(st_scr[pl.ds(chunk_off(2 * j + e), KEY_CHUNK), :].reshape(n32, SUBLANES, tq) == t8[None],
                             1.0, 0.0).reshape(KEY_CHUNK, tq).astype(_bf16) for e in range(2)]
            tie_scr[slot] = jnp.dot(tri_ref[...], jnp.concatenate(ind, axis=1), preferred_element_type=_f32)

        def tie_finish(j, slot, before):
            both = tie_scr[slot]
            for e in range(2):
                off = chunk_off(2 * j + e)
                st = st_scr[pl.ds(off, KEY_CHUNK), :].reshape(n32, SUBLANES, tq)
                within = both[:, e * tq:(e + 1) * tq].reshape(n32, SUBLANES, tq)
                drop = (st == t8[None]) & (within + before[None] > need_f[None])
                st_scr[pl.ds(off, KEY_CHUNK), :] = jnp.where(drop, NEG_INF, st).reshape(KEY_CHUNK, tq)
                before = before + jnp.broadcast_to(within[n32 - 1, SUBLANES - 1:, :], (SUBLANES, tq))
            return before

        def tie_body(j, before):
            before = tie_finish(j, j % 2, before)
            tie_dots(jnp.minimum(j + 1, nsl - 1), (j + 1) % 2)
            return before

        tie_dots(0, 0)
        lax.fori_loop(0, nsl, tie_body, jnp.zeros((SUBLANES, tq), _f32))

    tsel = jnp.maximum(t8, SEL_FLOOR)

    def chunk_update(c, raw_scr, m_old):
        off = chunk_off(c)
        sel = st_scr[pl.ds(off, KEY_CHUNK), :].reshape(n32, SUBLANES, tq) >= tsel[None]
        vc = vt_ref[0:V_GROUP, pl.ds(off, KEY_CHUNK)]
        m_new = []
        for pair in range(SA_HEADS // 2):
            ps, alphas = [], []
            for h in (2 * pair, 2 * pair + 1):
                sh = jnp.where(sel, raw_scr[h].reshape(n32, SUBLANES, tq), NEG_INF)
                mh = jnp.maximum(m_old[h], _sublane_all(_tree([sh[j] for j in range(n32)], jnp.maximum), jnp.maximum))
                ps.append(jnp.exp2(sh - mh[None]).reshape(KEY_CHUNK, tq).astype(_bf16))
                alphas.append(jnp.exp2(m_old[h] - mh)[:1])
                m_new.append(mh)
            acc_scr[:, pair_cols(pair)] = (jnp.concatenate(alphas, axis=1) * acc_scr[:, pair_cols(pair)]
                                           + jnp.dot(vc, jnp.concatenate(ps, axis=1), preferred_element_type=_f32))
        return m_new

    def attn_body(j, m):
        score_dots(2 * j + 1, rawb_scr)
        m = chunk_update(2 * j, rawa_scr, list(m))
        m = chunk_update(2 * j + 1, rawb_scr, m)
        score_dots(jnp.minimum(2 * j + 2, last_chunk), rawa_scr)
        return tuple(m)

    lax.fori_loop(0, nsl, attn_body, tuple(jnp.full((SUBLANES, tq), NEG_INF, _f32) for _ in range(SA_HEADS)))

    acc = acc_scr[...]
    store_heads(osp_ref, [acc[0:HEAD_DIM, h * tq:(h + 1) * tq] / acc[HEAD_DIM:HEAD_DIM + 1, h * tq:(h + 1) * tq]
                          for h in range(SA_HEADS)])


def _attention(sinks, qt, wit, k, vt, *, topk):
    B, _, S = qt.shape
    nkp = k.shape[1]
    tq = BLOCK
    kernel = functools.partial(_attn_kernel, topk=topk)
    out_blk = pl.BlockSpec((None, tq, SA_HEADS * HEAD_DIM), lambda b, i: (b, i, 0))
    return pl.pallas_call(
        kernel,
        out_shape=(jax.ShapeDtypeStruct((B, S, SA_HEADS * HEAD_DIM), _bf16),
                   jax.ShapeDtypeStruct((B, S, SW_HEADS * HEAD_DIM), _bf16)),
        grid=(B, S // tq),
        in_specs=[
            pl.BlockSpec(memory_space=pltpu.SMEM),
            pl.BlockSpec((None, QT_ROWS, tq), lambda b, i: (b, 0, i)),
            pl.BlockSpec((None, WI_ROWS, tq), lambda b, i: (b, 0, i)),
            pl.BlockSpec((None, nkp, K_COLS), lambda b, i: (b, 0, 0)),
            pl.BlockSpec((None, VT_ROWS, nkp), lambda b, i: (b, 0, 0)),
            _const_spec((KEY_CHUNK, KEY_CHUNK)),
        ],
        out_specs=(out_blk, out_blk),
        scratch_shapes=[
            pltpu.VMEM((nkp, tq), _f32),
            pltpu.VMEM((2 * HEAD_DIM, SA_HEADS * tq), _bf16),
            pltpu.VMEM((2 * HEAD_DIM, SW_HEADS * tq), _bf16),
            pltpu.VMEM((SA_HEADS, KEY_CHUNK, tq), _f32),
            pltpu.VMEM((SA_HEADS, KEY_CHUNK, tq), _f32),
            pltpu.VMEM((2, KEY_CHUNK, 2 * tq), _f32),
            pltpu.VMEM((V_GROUP, SA_HEADS * tq), _f32),
        ],
        compiler_params=pltpu.CompilerParams(
            dimension_semantics=("arbitrary", "arbitrary"),
            vmem_limit_bytes=VMEM_LIMIT_BYTES),
        name="attn",
    )(sinks, qt, wit, k, vt, jnp.tril(jnp.ones((KEY_CHUNK, KEY_CHUNK), _bf16)))


def _back_kernel(h1_ref, osp_ref, osw_ref, gate_ref, wbs_ref, wbw_ref, wo_ref, g2_ref,
                 wg_ref, wu_ref, wd_ref, gf_ref, out_ref):
    tm, d = h1_ref.shape
    hm = tm // _row_splits(tm)
    for r in range(tm // hm):
        rows = slice(r * hm, (r + 1) * hm)
        a = jnp.dot(osp_ref[rows, :], wbs_ref[...], preferred_element_type=_f32)
        b = jnp.dot(osw_ref[rows, :], wbw_ref[...], preferred_element_type=_f32)
        gates = gate_ref[rows, :]
        merged = (gates[:, :d] * a + gates[:, d:] * b).astype(_bf16)
        h2 = h1_ref[rows, :] + jnp.dot(merged, wo_ref[...], preferred_element_type=_f32)
        h3 = _swiglu_half_step(h2, g2_ref[...], wg_ref, wu_ref, wd_ref)
        out_ref[rows, :] = _rms(h3, gf_ref[...])


def _back(h1, osp, osw, gates, wbs, wbw, wo, g2, wg, wu, wd, gf, *, tm):
    B, S, D = h1.shape
    F = wg.shape[1]
    row_blk = lambda w: pl.BlockSpec((None, tm, w), lambda b, j: (b, j, 0))
    return pl.pallas_call(
        _back_kernel,
        out_shape=jax.ShapeDtypeStruct((B, S, D), _f32),
        grid=(B, S // tm),
        in_specs=[
            row_blk(D), row_blk(osp.shape[-1]), row_blk(osw.shape[-1]), row_blk(gates.shape[-1]),
            _const_spec(wbs.shape), _const_spec(wbw.shape), _const_spec(wo.shape), _const_spec((1, D)),
            _const_spec((D, F)), _const_spec((D, F)), _const_spec((F, D)), _const_spec((1, D)),
        ],
        out_specs=row_blk(D),
        compiler_params=pltpu.CompilerParams(
            dimension_semantics=("arbitrary", "arbitrary"),
            vmem_limit_bytes=VMEM_LIMIT_BYTES),
        name="back",
    )(h1, osp, osw, gates, wbs, wbw, wo, g2, wg, wu, wd, gf)


def _rope_tables(rpos):
    inv_freq = 1.0 / (ROPE_THETA ** (jnp.arange(0, HEAD_DIM, 2, dtype=_f32) / HEAD_DIM))
    ang = rpos.astype(_f32)[:, None] * inv_freq[None, :]
    cos, sin = jnp.cos(ang), jnp.sin(ang)
    cos64 = jnp.concatenate([cos, cos], axis=1)
    sin64 = jnp.concatenate([-sin, sin], axis=1)
    reps = LANES // HEAD_DIM
    return (jnp.tile(cos64, (1, reps)), jnp.tile(sin64, (1, reps)), cos64.T, sin64.T)


def _pack_w_in(w):
    sizes = (SA_HEADS * HEAD_DIM, HEAD_DIM, HEAD_DIM, IDX_HEADS * IDX_DIM, IDX_DIM, IDX_HEADS,
             SW_HEADS * HEAD_DIM, SW_KV_HEADS * HEAD_DIM, SW_KV_HEADS * HEAD_DIM)
    d = w.shape[0]
    gate_cols = w.shape[1] - sum(sizes)
    splits = [int(s) for s in np.cumsum(sizes + (gate_cols,))[:-1]]
    qa, ka, va, qi, ki, wi, qs, ksw, vsw, gates = jnp.split(w, splits, axis=1)
    scale = HEAD_DIM ** -0.5
    idx_scale = (IDX_HEADS ** -0.5) * (IDX_DIM ** -0.5)
    zeros = lambda n: jnp.zeros((d, n), w.dtype)
    wrow = jnp.concatenate([ka, ki, ksw, gates], axis=1)
    pad = zeros(V_GROUP - HEAD_DIM)
    wt = jnp.concatenate(
        [qa * scale, qs * scale, qi, va, pad]
        + [blk for g in range(SW_KV_HEADS) for blk in (vsw[:, g * HEAD_DIM:(g + 1) * HEAD_DIM], pad)]
        + [wi * idx_scale, zeros(WI_ROWS - wi.shape[1])], axis=1).T
    return wrow.astype(_bf16), wt.astype(_bf16)


def kernel(x, meta_tokens, norm_ffn1, w_ffn1_gate, w_ffn1_up, w_ffn1_down, norm_mix, w_in, sinks,
           w_branch_sparse, w_branch_swa, w_out, norm_ffn2, w_ffn2_gate, w_ffn2_up, w_ffn2_down, norm_final):
    B, S, D = x.shape
    assert norm_ffn1.shape[0] == 1, "single-layer stack only"
    assert S % BLOCK == 0
    topk = min(TOPK_MAX, S // 4)
    tm = next(t for t in (ROW_SPLITS * 2 * LANES, 2 * LANES, BLOCK) if S % t == 0)
    bf = lambda a: a.astype(_bf16)

    wrow, wt = _pack_w_in(w_in[0])
    g1, gm, g2 = norm_ffn1[0][None], norm_mix[0][None], norm_ffn2[0][None]
    wg1, wu1, wd1 = bf(w_ffn1_gate[0]), bf(w_ffn1_up[0]), bf(w_ffn1_down[0])

    pos_real = N_META + jnp.arange(S, dtype=jnp.int32)
    pos_meta = jnp.maximum(jnp.arange(BLOCK, dtype=jnp.int32) - N_PAD, 0)
    h1, k, gates, qt, vt, wit = _front(x, _rope_tables(pos_real), g1, gm, wg1, wu1, wd1, wrow, wt, tm=tm)
    meta_blk = jnp.concatenate([jnp.zeros((N_PAD, D), x.dtype), meta_tokens.astype(x.dtype)], axis=0)[None]
    _, k_m, _, _, vt_m, _ = _front(meta_blk, _rope_tables(pos_meta), g1, gm, wg1, wu1, wd1, wrow, wt, tm=BLOCK)

    nkp = -(-(S + BLOCK) // SEARCH_CHUNK) * SEARCH_CHUNK
    k_all = jnp.concatenate([jnp.broadcast_to(k_m, (B, BLOCK, K_COLS)), k,
                             jnp.zeros((B, nkp - S - BLOCK, K_COLS), k.dtype)], axis=1)
    vt_all = jnp.concatenate([jnp.broadcast_to(vt_m, (B, VT_ROWS, BLOCK)), vt,
                              jnp.zeros((B, VT_ROWS, nkp - S - BLOCK), vt.dtype)], axis=2)

    o_sparse, o_swa = _attention(sinks[0].astype(_f32), qt, wit, k_all, vt_all, topk=topk)

    return _back(h1, o_sparse, o_swa, gates, bf(w_branch_sparse[0]), bf(w_branch_swa[0]), bf(w_out[0]), g2,
                 bf(w_ffn2_gate[0]), bf(w_ffn2_up[0]), bf(w_ffn2_down[0]), norm_final[None], tm=tm)
```
